```python
import math
import jax, jax.numpy as jnp
from jax import lax
import numpy as np

D_MODEL = 1024
BATCH = 4
SEQ = 8192
DEPTH = 2

MIX_WIDTH = D_MODEL
CONV_A_WIDTH = MIX_WIDTH // 4
DIFF_WIDTH = MIX_WIDTH // 2
CONF_WIDTH = MIX_WIDTH - CONV_A_WIDTH - DIFF_WIDTH
N_DIFF_HEADS = 4
DIFF_HEAD_DIM = DIFF_WIDTH // (2 * N_DIFF_HEADS)
DIFF_V_DIM = 2 * DIFF_HEAD_DIM
SHORT_CONV_K = 3
CONF_CONV_K = 31
ROPE_THETA = 10000.0
Q_BLOCK = 128
FFN_HIDDEN = -(-8 * D_MODEL // (3 * 256)) * 256
RMS_EPS = 1e-6
LN_EPS = 1e-5
IN_SPLITS = [CONV_A_WIDTH, CONV_A_WIDTH, CONV_A_WIDTH,
             DIFF_WIDTH, DIFF_WIDTH, DIFF_WIDTH,
             2 * CONF_WIDTH]
IN_WIDTH = sum(IN_SPLITS)

kernel_name = "hybrid_shortconv_diffattn_conformer"


def rmsnorm(x, g):
    xf = x.astype(jnp.float32)
    y = xf * lax.rsqrt(jnp.mean(xf * xf, axis=-1, keepdims=True) + RMS_EPS)
    return (y * g.astype(jnp.float32)).astype(x.dtype)


def layernorm(x, g, b):
    xf = x.astype(jnp.float32)
    mu = jnp.mean(xf, axis=-1, keepdims=True)
    xc = xf - mu
    y = xc * lax.rsqrt(jnp.mean(xc * xc, axis=-1, keepdims=True) + LN_EPS)
    return (y * g.astype(jnp.float32) + b.astype(jnp.float32)).astype(x.dtype)


def causal_depthwise_conv(x, w):
    k, c = w.shape
    return lax.conv_general_dilated(
        x, w.astype(x.dtype)[:, None, :], window_strides=(1,), padding=((k - 1, 0),),
        dimension_numbers=("NWC", "WIO", "NWC"), feature_group_count=c)


def rope_tables(positions):
    inv_freq = 1.0 / (ROPE_THETA ** (jnp.arange(0, DIFF_HEAD_DIM, 2, dtype=jnp.float32) / DIFF_HEAD_DIM))
    ang = positions.astype(jnp.float32)[..., None] * inv_freq
    return jnp.cos(ang)[:, :, None, None, :], jnp.sin(ang)[:, :, None, None, :]


def apply_rope(x, cos, sin):
    xf = x.astype(jnp.float32)
    x1, x2 = jnp.split(xf, 2, axis=-1)
    out = jnp.concatenate([x1 * cos - x2 * sin, x2 * cos + x1 * sin], axis=-1)
    return out.astype(x.dtype)


def diff_attention(q, k, v, lam):
    b, s, _, h, d = q.shape
    nb = s // Q_BLOCK
    qb = q.reshape(b, nb, Q_BLOCK, 2, h, d).transpose(1, 0, 2, 3, 4, 5)
    key_idx = jnp.arange(s)
    scale = d ** -0.5

    def block(args):
        i, qi = args
        sc = jnp.einsum("bqmhd,bkmhd->bmhqk", qi, k,
                        preferred_element_type=jnp.float32) * scale
        q_idx = i * Q_BLOCK + jnp.arange(Q_BLOCK)
        mask = key_idx[None, :] <= q_idx[:, None]
        p = jax.nn.softmax(jnp.where(mask, sc, -jnp.inf), axis=-1)
        a = p[:, 0] - lam * p[:, 1]
        return jnp.einsum("bhqk,bkhe->bqhe", a.astype(v.dtype), v)

    out = lax.map(block, (jnp.arange(nb), qb))
    return out.transpose(1, 0, 2, 3, 4).reshape(b, s, h, DIFF_V_DIM)


def mixing_layer(h, cos, sin, w_in, short_conv_w, glu_b, conf_dw_w, conf_dw_b,
                 conf_ln_g, conf_ln_b, lam_q1, lam_k1, lam_q2, lam_k2, diff_subln_g,
                 w_out, lam_init):
    b, s, _ = h.shape
    u = jnp.einsum("bsd,de->bse", h, w_in)
    offs = np.cumsum(IN_SPLITS)[:-1].tolist()
    a_b, a_c, a_x, q, k, v, c_in = jnp.split(u, offs, axis=-1)

    y_a = a_b * causal_depthwise_conv(a_c * a_x, short_conv_w)

    q = apply_rope(q.reshape(b, s, 2, N_DIFF_HEADS, DIFF_HEAD_DIM), cos, sin)
    k = apply_rope(k.reshape(b, s, 2, N_DIFF_HEADS, DIFF_HEAD_DIM), cos, sin)
    v = v.reshape(b, s, N_DIFF_HEADS, DIFF_V_DIM)
    lam = (jnp.exp(jnp.sum(lam_q1.astype(jnp.float32) * lam_k1.astype(jnp.float32)))
           - jnp.exp(jnp.sum(lam_q2.astype(jnp.float32) * lam_k2.astype(jnp.float32)))
           + lam_init)
    o = diff_attention(q, k, v, lam)
    o = rmsnorm(o, diff_subln_g) * (1.0 - lam_init)
    y_b = o.reshape(b, s, DIFF_WIDTH)

    c = c_in + glu_b
    c = c[..., :CONF_WIDTH] * jax.nn.sigmoid(c[..., CONF_WIDTH:])
    c = causal_depthwise_conv(c, conf_dw_w) + conf_dw_b
    y_c = jax.nn.silu(layernorm(c, conf_ln_g, conf_ln_b))

    y = jnp.concatenate([y_a, y_b, y_c], axis=-1)
    return jnp.einsum("bse,ed->bsd", y, w_out)


def swiglu(h, w_gate, w_up, w_down):
    g = jnp.einsum("bsd,df->bsf", h, w_gate)
    u = jnp.einsum("bsd,df->bsf", h, w_up)
    return jnp.einsum("bsf,fd->bsd", jax.nn.silu(g) * u, w_down)


def setup_inputs(seed: int = 0) -> dict:
    key = jax.random.key(seed)
    ks = iter(jax.random.split(key, 32))
    f32 = jnp.float32

    def nrm(shape, scale):
        return jax.random.normal(next(ks), shape, f32) * scale

    x = jax.random.normal(next(ks), (BATCH, SEQ, D_MODEL), f32)
    offsets = jax.random.randint(next(ks), (BATCH, 1), 0, 1024, dtype=jnp.int32)
    positions = offsets + jnp.arange(SEQ, dtype=jnp.int32)[None, :]
    return {
        "x": x,
        "positions": positions,
        "mix_norm_g": 1.0 + nrm((DEPTH, D_MODEL), 0.02),
        "w_in": nrm((DEPTH, D_MODEL, IN_WIDTH), D_MODEL ** -0.5),
        "short_conv_w": nrm((DEPTH, SHORT_CONV_K, CONV_A_WIDTH), SHORT_CONV_K ** -0.5),
        "glu_b": nrm((DEPTH, 2 * CONF_WIDTH), 0.02),
        "conf_dw_w": nrm((DEPTH, CONF_CONV_K, CONF_WIDTH), CONF_CONV_K ** -0.5),
        "conf_dw_b": nrm((DEPTH, CONF_WIDTH), 0.02),
        "conf_ln_g": 1.0 + nrm((DEPTH, CONF_WIDTH), 0.02),
        "conf_ln_b": nrm((DEPTH, CONF_WIDTH), 0.02),
        "lam_q1": nrm((DEPTH, DIFF_HEAD_DIM), 0.1),
        "lam_k1": nrm((DEPTH, DIFF_HEAD_DIM), 0.1),
        "lam_q2": nrm((DEPTH, DIFF_HEAD_DIM), 0.1),
        "lam_k2": nrm((DEPTH, DIFF_HEAD_DIM), 0.1),
        "diff_subln_g": 1.0 + nrm((DEPTH, DIFF_V_DIM), 0.02),
        "w_out": nrm((DEPTH, MIX_WIDTH, D_MODEL), MIX_WIDTH ** -0.5),
        "ffn_norm_g": 1.0 + nrm((DEPTH, D_MODEL), 0.02),
        "w_gate": nrm((DEPTH, D_MODEL, FFN_HIDDEN), D_MODEL ** -0.5),
        "w_up": nrm((DEPTH, D_MODEL, FFN_HIDDEN), D_MODEL ** -0.5),
        "w_down": nrm((DEPTH, FFN_HIDDEN, D_MODEL), FFN_HIDDEN ** -0.5),
        "final_norm_g": 1.0 + nrm((D_MODEL,), 0.02),
    }


def reference(x, positions, mix_norm_g, w_in, short_conv_w, glu_b, conf_dw_w, conf_dw_b,
              conf_ln_g, conf_ln_b, lam_q1, lam_k1, lam_q2, lam_k2, diff_subln_g, w_out,
              ffn_norm_g, w_gate, w_up, w_down, final_norm_g):
    cos, sin = rope_tables(positions)
    for l in range(DEPTH):
        lam_init = 0.8 - 0.6 * math.exp(-0.3 * l)
        h = rmsnorm(x, mix_norm_g[l])
        x = x + mixing_layer(h, cos, sin, w_in[l], short_conv_w[l], glu_b[l], conf_dw_w[l],
                             conf_dw_b[l], conf_ln_g[l], conf_ln_b[l], lam_q1[l], lam_k1[l],
                             lam_q2[l], lam_k2[l], diff_subln_g[l], w_out[l], lam_init)
        h = rmsnorm(x, ffn_norm_g[l])
        x = x + swiglu(h, w_gate[l], w_up[l], w_down[l])
    return rmsnorm(x, final_norm_g)
```

```python
import functools
import math

import jax
import jax.numpy as jnp
from jax import lax
from jax.experimental import pallas as pl
from jax.experimental.pallas import tpu as pltpu

F32 = jnp.float32
BF16 = jnp.bfloat16

N_HEADS = 4
HEAD_DIM = 64
V_DIM = 2 * HEAD_DIM
SHORT_K = 3
CONF_K = 31
ROPE_THETA = 10000.0
RMS_EPS = 1e-6
LN_EPS = 1e-5

LANES = 128
SUBLANES = 8
BF16_ROWS = 16
VMEM_LIMIT = 56 * 1024 * 1024

TM_PROJ = 512
TQ = 256
TK = 256
CONV_ROWS = 32
FFN_CHUNK = 256
SHORT_HALO = SUBLANES
CONF_HALO = 32
V_ROWS = V_DIM + BF16_ROWS
MASK_VALUE = -1e30


def _rms(x, g):
    return x * lax.rsqrt(jnp.mean(x * x, axis=-1, keepdims=True) + RMS_EPS) * g


def _dot(a, b):
    return jnp.dot(a, b, preferred_element_type=F32)


def _proj_in_kernel(x_ref, g_ref, cos_ref, sin_ref, w_ref, glub_ref,
                    ab_ref, acx_ref, glu_ref, qT_ref, k_ref, vT_ref, *, widths, tk):
    ca, diff, conf = widths
    tm = x_ref.shape[1]
    h = _rms(x_ref[0], g_ref[...]).astype(BF16)

    u_a = _dot(h, w_ref[:, 0:3 * ca])
    ab_ref[0] = u_a[:, 0:ca]
    acx_ref[0] = u_a[:, ca:2 * ca] * u_a[:, 2 * ca:3 * ca]

    cos = cos_ref[0]
    sin = sin_ref[0]
    lane = lax.broadcasted_iota(jnp.int32, (tm, LANES), 1)
    lower_half = (lane % HEAD_DIM) < (HEAD_DIM // 2)

    def rope(u):
        partner = jnp.where(lower_half, pltpu.roll(u, LANES - HEAD_DIM // 2, 1),
                            pltpu.roll(u, HEAD_DIM // 2, 1))
        return u * cos + partner * sin

    off = 3 * ca
    scale = HEAD_DIM ** -0.5
    u_q = _dot(h, w_ref[:, off:off + diff])
    for hh in range(N_HEADS):
        q_h = rope(u_q[:, hh * LANES:(hh + 1) * LANES]) * scale
        qT_ref[0, hh] = q_h.T.astype(BF16)
    off += diff
    u_k = _dot(h, w_ref[:, off:off + diff])
    for hh in range(N_HEADS):
        k_ref[0, :, hh * LANES:(hh + 1) * LANES] = rope(u_k[:, hh * LANES:(hh + 1) * LANES]).astype(BF16)
    off += diff
    u_v = _dot(h, w_ref[:, off:off + diff])
    ones = jnp.ones((BF16_ROWS, tk), BF16)
    for hh in range(N_HEADS):
        v_hT = u_v[:, hh * V_DIM:(hh + 1) * V_DIM].T.astype(BF16)
        for t in range(tm // tk):
            vT_ref[0, hh, t, 0:V_DIM, :] = v_hT[:, t * tk:(t + 1) * tk]
            vT_ref[0, hh, t, V_DIM:V_ROWS, :] = ones
    off += diff
    u_c = _dot(h, w_ref[:, off:off + 2 * conf]) + glub_ref[...]
    glu_ref[0] = u_c[:, 0:conf] * jax.nn.sigmoid(u_c[:, conf:2 * conf])


def _proj_in(x, g, cos_t, sin_t, w_in, glu_b, *, tm, tk):
    b, s, d = x.shape
    ca, diff = d // 4, d // 2
    conf = d - ca - diff
    in_w = w_in.shape[1]
    nt = s // tm
    tok = lambda width: pl.BlockSpec((1, tm, width), lambda bi, i: (bi, i, 0))
    row = lambda width: pl.BlockSpec((1, width), lambda bi, i: (0, 0))
    return pl.pallas_call(
        functools.partial(_proj_in_kernel, widths=(ca, diff, conf), tk=tk),
        grid=(b, nt),
        in_specs=[tok(d), row(d), tok(LANES), tok(LANES),
                  pl.BlockSpec((d, in_w), lambda bi, i: (0, 0)), row(2 * conf)],
        out_specs=[tok(ca), tok(ca), tok(conf),
                   pl.BlockSpec((1, N_HEADS, 2 * HEAD_DIM, tm), lambda bi, i: (bi, 0, 0, i)),
                   tok(diff),
                   pl.BlockSpec((1, N_HEADS, tm // tk, V_ROWS, tk), lambda bi, i: (bi, 0, i, 0, 0))],
        out_shape=[jax.ShapeDtypeStruct((b, s, ca), F32),
                   jax.ShapeDtypeStruct((b, s, ca), F32),
                   jax.ShapeDtypeStruct((b, s, conf), F32),
                   jax.ShapeDtypeStruct((b, N_HEADS, 2 * HEAD_DIM, s), BF16),
                   jax.ShapeDtypeStruct((b, s, diff), BF16),
                   jax.ShapeDtypeStruct((b, N_HEADS, s // tk, V_ROWS, tk), BF16)],
        compiler_params=pltpu.CompilerParams(
            dimension_semantics=("arbitrary", "arbitrary"), vmem_limit_bytes=VMEM_LIMIT),
        name="proj_in",
    )(x, g, cos_t, sin_t, w_in, glu_b)


def _attn_kernel(lq1_ref, lk1_ref, lq2_ref, lk2_ref, g_ref, qT_ref, k_ref, vT_ref,
                 o_ref, m_ref, acc_ref, *, tq, tk, lam_init):
    i = pl.program_id(2)
    qT = qT_ref[0, 0]
    feat = lax.broadcasted_iota(jnp.int32, qT.shape, 0)
    zero = jnp.zeros_like(qT)
    q2T = jnp.concatenate([jnp.where(feat < HEAD_DIM, qT, zero),
                           jnp.where(feat >= HEAD_DIM, qT, zero)], axis=1)

    m_ref[...] = jnp.full(m_ref.shape, MASK_VALUE, F32)
    acc_ref[...] = jnp.zeros(acc_ref.shape, F32)

    def step(j, masked):
        k_j = k_ref[0, pl.ds(pl.multiple_of(j * tk, tk), tk), :]
        sT = _dot(k_j, q2T)
        if masked:
            key = j * tk + lax.broadcasted_iota(jnp.int32, sT.shape, 0)
            col = lax.broadcasted_iota(jnp.int32, sT.shape, 1)
            query = i * tq + jnp.where(col >= tq, col - tq, col)
            sT = jnp.where(key <= query, sT, MASK_VALUE)
        m_old = m_ref[...]
        m_new = jnp.maximum(m_old, jnp.max(sT, axis=0, keepdims=True))
        pT = jnp.exp(sT - m_new).astype(BF16)
        acc_ref[...] = jnp.exp(m_old - m_new) * acc_ref[...] + _dot(vT_ref[0, 0, j], pT)
        m_ref[...] = m_new

    n_full = i * (tq // tk)

    def body(j, carry):
        step(j, masked=False)
        return carry

    lax.fori_loop(0, n_full, body, 0)
    for dj in range(tq // tk):
        step(n_full + dj, masked=True)

    acc = acc_ref[...]
    oT = acc[0:V_DIM, :] / acc[V_DIM:V_DIM + 1, :]
    lam = (jnp.exp(jnp.sum(lq1_ref[...] * lk1_ref[...], keepdims=True))
           - jnp.exp(jnp.sum(lq2_ref[...] * lk2_ref[...], keepdims=True)) + lam_init)
    o = (oT[:, 0:tq] - lam * oT[:, tq:2 * tq]).T
    o_ref[0] = (_rms(o, g_ref[...]) * (1.0 - lam_init)).astype(o_ref.dtype)


def _attn(qT, k, vT, lq1, lk1, lq2, lk2, subln_g, *, tq, tk, lam_init):
    b, nh, _, s = qT.shape
    nk = s // tk
    row = lambda width: pl.BlockSpec((1, width), lambda bi, hi, i: (0, 0))
    return pl.pallas_call(
        functools.partial(_attn_kernel, tq=tq, tk=tk, lam_init=lam_init),
        grid=(b, nh, s // tq),
        in_specs=[row(HEAD_DIM), row(HEAD_DIM), row(HEAD_DIM), row(HEAD_DIM), row(V_DIM),
                  pl.BlockSpec((1, 1, 2 * HEAD_DIM, tq), lambda bi, hi, i: (bi, hi, 0, i)),
                  pl.BlockSpec((1, s, 2 * HEAD_DIM), lambda bi, hi, i: (bi, 0, hi)),
                  pl.BlockSpec((1, 1, nk, V_ROWS, tk), lambda bi, hi, i: (bi, hi, 0, 0, 0))],
        out_specs=pl.BlockSpec((1, tq, V_DIM), lambda bi, hi, i: (bi, i, hi)),
        out_shape=jax.ShapeDtypeStruct((b, s, nh * V_DIM), BF16),
        scratch_shapes=[pltpu.VMEM((1, 2 * tq), F32), pltpu.VMEM((V_ROWS, 2 * tq), F32)],
        compiler_params=pltpu.CompilerParams(
            dimension_semantics=("arbitrary", "arbitrary", "arbitrary"),
            vmem_limit_bytes=VMEM_LIMIT),
        name="attn",
    )(lq1, lk1, lq2, lk2, subln_g, qT, k, vT)


def _causal_dwconv(win_ref, w_ref, halo, taps, tm, emit):
    base = halo - (taps - 1)
    for r0 in range(0, tm, CONV_ROWS):
        acc = w_ref[0:1, :] * win_ref[pl.ds(base + r0, CONV_ROWS), :]
        for j in range(1, taps):
            acc = acc + w_ref[j:j + 1, :] * win_ref[pl.ds(base + r0 + j, CONV_ROWS), :]
        emit(r0, CONV_ROWS, acc)


def _mix_out_kernel(x_ref, ab_ref, acx_ref, acx_halo_ref, glu_ref, glu_halo_ref, yb_ref,
                    scw_ref, dww_ref, dwb_ref, lng_ref, lnb_ref, wout_ref,
                    o_ref, wina_ref, winc_ref, y_ref, *, widths):
    ca, diff, conf = widths
    tm = x_ref.shape[1]
    first = pl.program_id(1) == 0

    wina_ref[0:SHORT_HALO, :] = jnp.where(first, 0.0, acx_halo_ref[0])
    wina_ref[SHORT_HALO:SHORT_HALO + tm, :] = acx_ref[0]
    winc_ref[0:CONF_HALO, :] = jnp.where(first, 0.0, glu_halo_ref[0])
    winc_ref[CONF_HALO:CONF_HALO + tm, :] = glu_ref[0]

    def emit_a(r0, rows, conv):
        y_ref[pl.ds(r0, rows), 0:ca] = (ab_ref[0, pl.ds(r0, rows), :] * conv).astype(BF16)

    _causal_dwconv(wina_ref, scw_ref, SHORT_HALO, SHORT_K, tm, emit_a)

    y_ref[:, ca:ca + diff] = yb_ref[0]

    def emit_c(r0, rows, conv):
        c = conv + dwb_ref[...]
        mu = jnp.mean(c, axis=-1, keepdims=True)
        cc = c - mu
        ln = cc * lax.rsqrt(jnp.mean(cc * cc, axis=-1, keepdims=True) + LN_EPS)
        ln = ln * lng_ref[...] + lnb_ref[...]
        y_ref[pl.ds(r0, rows), ca + diff:ca + diff + conf] = (ln * jax.nn.sigmoid(ln)).astype(BF16)

    _causal_dwconv(winc_ref, dww_ref, CONF_HALO, CONF_K, tm, emit_c)

    o_ref[0] = x_ref[0] + _dot(y_ref[...], wout_ref[...])


def _mix_out(x, ab, acx, glu, yb, scw, dww, dwb, lng, lnb, w_out, *, tm):
    b, s, d = x.shape
    ca, diff = d // 4, d // 2
    conf = d - ca - diff
    tok = lambda width: pl.BlockSpec((1, tm, width), lambda bi, i: (bi, i, 0))
    halo = lambda rows, width: pl.BlockSpec(
        (1, rows, width), lambda bi, i: (bi, jnp.maximum(i * (tm // rows) - 1, 0), 0))
    full = lambda r, c: pl.BlockSpec((r, c), lambda bi, i: (0, 0))
    return pl.pallas_call(
        functools.partial(_mix_out_kernel, widths=(ca, diff, conf)),
        grid=(b, s // tm),
        in_specs=[tok(d), tok(ca), tok(ca), halo(SHORT_HALO, ca), tok(conf), halo(CONF_HALO, conf),
                  tok(diff), full(SHORT_K, ca), full(CONF_K, conf), full(1, conf), full(1, conf),
                  full(1, conf), full(d, d)],
        out_specs=tok(d),
        out_shape=jax.ShapeDtypeStruct((b, s, d), F32),
        scratch_shapes=[pltpu.VMEM((SHORT_HALO + tm, ca), F32),
                        pltpu.VMEM((CONF_HALO + tm, conf), F32),
                        pltpu.VMEM((tm, d), BF16)],
        compiler_params=pltpu.CompilerParams(
            dimension_semantics=("arbitrary", "arbitrary"), vmem_limit_bytes=VMEM_LIMIT),
        name="mix_out",
    )(x, ab, acx, acx, glu, glu, yb, scw, dww, dwb, lng, lnb, w_out)


def _ffn_kernel(x_ref, g_ref, wg_ref, wu_ref, wd_ref, fg_ref, o_ref, act_ref, *, final):
    x = x_ref[...]
    h = _rms(x, g_ref[...]).astype(BF16)
    hidden = wg_ref.shape[1]
    for c0 in range(0, hidden, FFN_CHUNK):
        gate = _dot(h, wg_ref[:, c0:c0 + FFN_CHUNK])
        up = _dot(h, wu_ref[:, c0:c0 + FFN_CHUNK])
        act_ref[:, c0:c0 + FFN_CHUNK] = (gate * jax.nn.sigmoid(gate) * up).astype(BF16)
    y = x + _dot(act_ref[...], wd_ref[...])
    if final:
        y = _rms(y, fg_ref[...])
    o_ref[...] = y


def _ffn(x2d, g, w_gate, w_up, w_down, final_g, *, tm, final):
    t, d = x2d.shape
    hidden = w_gate.shape[1]
    tok = pl.BlockSpec((tm, d), lambda i: (i, 0))
    const = lambda r, c: pl.BlockSpec((r, c), lambda i: (0, 0), pipeline_mode=pl.Buffered(1))
    return pl.pallas_call(
        functools.partial(_ffn_kernel, final=final),
        grid=(t // tm,),
        in_specs=[tok, const(1, d), const(d, hidden), const(d, hidden), const(hidden, d), const(1, d)],
        out_specs=tok,
        out_shape=jax.ShapeDtypeStruct((t, d), F32),
        scratch_shapes=[pltpu.VMEM((tm, hidden), BF16)],
        compiler_params=pltpu.CompilerParams(
            dimension_semantics=("arbitrary",), vmem_limit_bytes=VMEM_LIMIT),
        name="ffn",
    )(x2d, g, w_gate, w_up, w_down, final_g)


def _rope_tables(positions):
    half = HEAD_DIM // 2
    inv_freq = 1.0 / (ROPE_THETA ** (jnp.arange(0, HEAD_DIM, 2, dtype=F32) / HEAD_DIM))
    ang = positions.astype(F32)[..., None] * inv_freq
    cos, sin = jnp.cos(ang), jnp.sin(ang)
    reps = LANES // half
    cos_t = jnp.tile(cos, (1, 1, reps))
    sin_t = jnp.tile(jnp.concatenate([-sin, sin], axis=-1), (1, 1, reps // 2))
    return cos_t, sin_t


def _head_major(w, d, ca, diff):
    def reorder(cols):
        return cols.reshape(d, 2, N_HEADS, HEAD_DIM).transpose(0, 2, 1, 3).reshape(d, diff)
    q0 = 3 * ca
    return jnp.concatenate(
        [w[:, :q0], reorder(w[:, q0:q0 + diff]), reorder(w[:, q0 + diff:q0 + 2 * diff]),
         w[:, q0 + 2 * diff:]], axis=1)


def kernel(x, positions, mix_norm_g, w_in, short_conv_w, glu_b, conf_dw_w, conf_dw_b, conf_ln_g, conf_ln_b, lam_q1, lam_k1, lam_q2, lam_k2, diff_subln_g, w_out, ffn_norm_g, w_gate, w_up, w_down, final_norm_g):
    b, s, d = x.shape
    depth = w_in.shape[0]
    ca, diff = d // 4, d // 2
    assert diff == 2 * N_HEADS * HEAD_DIM and N_HEADS * V_DIM == diff
    tm = min(TM_PROJ, s)
    tq, tk = min(TQ, s), min(TK, s)
    assert s % tm == 0 and s % tq == 0 and tq % tk == 0 and tm % tk == 0

    cos_t, sin_t = _rope_tables(positions)
    row = lambda v: v.reshape(1, -1)
    for l in range(depth):
        lam_init = 0.8 - 0.6 * math.exp(-0.3 * l)
        w_in_l = _head_major(w_in[l], d, ca, diff).astype(BF16)
        ab, acx, glu, qT, k, vT = _proj_in(x, row(mix_norm_g[l]), cos_t, sin_t, w_in_l,
                                          row(glu_b[l]), tm=tm, tk=tk)
        yb = _attn(qT, k, vT, row(lam_q1[l]), row(lam_k1[l]), row(lam_q2[l]), row(lam_k2[l]),
                   row(diff_subln_g[l]), tq=tq, tk=tk, lam_init=lam_init)
        x = _mix_out(x, ab, acx, glu, yb, short_conv_w[l], conf_dw_w[l], row(conf_dw_b[l]),
                     row(conf_ln_g[l]), row(conf_ln_b[l]), w_out[l].astype(BF16), tm=tm)
        x = _ffn(x.reshape(b * s, d), row(ffn_norm_g[l]), w_gate[l].astype(BF16),
                 w_up[l].astype(BF16), w_down[l].astype(BF16), row(final_norm_g),
                 tm=tm, final=(l == depth - 1)).reshape(b, s, d)
    return x
```

```python
import functools
import math

import jax
import jax.numpy as jnp
from jax import lax
from jax.experimental import pallas as pl
from jax.experimental.pallas import tpu as pltpu

F32 = jnp.float32
BF16 = jnp.bfloat16

N_HEADS = 4
HEAD_DIM = 64
V_DIM = 2 * HEAD_DIM
SHORT_K = 3
CONF_K = 31
ROPE_THETA = 10000.0
RMS_EPS = 1e-6
LN_EPS = 1e-5

LANES = 128
SUBLANES = 8
BF16_ROWS = 16
VMEM_LIMIT = 56 * 1024 * 1024

TM_PROJ = 512
TQ = 256
TK = 512
CONV_ROWS = 32
FFN_CHUNK = 256
SHORT_HALO = SUBLANES
CONF_HALO = 32
V_ROWS = V_DIM + BF16_ROWS
MASK_VALUE = -1e30


def _rms(x, g):
    return x * lax.rsqrt(jnp.mean(x * x, axis=-1, keepdims=True) + RMS_EPS) * g


def _dot(a, b):
    return jnp.dot(a, b, preferred_element_type=F32)


def _proj_in_kernel(x_ref, g_ref, cos_ref, sin_ref, w_ref, glub_ref,
                    ab_ref, acx_ref, glu_ref, qT_ref, k_ref, vT_ref, *, widths, tk):
    ca, diff, conf = widths
    tm = x_ref.shape[1]
    h = _rms(x_ref[0], g_ref[...]).astype(BF16)

    u_a = _dot(h, w_ref[:, 0:3 * ca])
    ab_ref[0] = u_a[:, 0:ca]
    acx_ref[0] = u_a[:, ca:2 * ca] * u_a[:, 2 * ca:3 * ca]

    cos = cos_ref[0]
    sin = sin_ref[0]
    lane = lax.broadcasted_iota(jnp.int32, (tm, LANES), 1)
    lower_half = (lane % HEAD_DIM) < (HEAD_DIM // 2)

    def rope(u):
        partner = jnp.where(lower_half, pltpu.roll(u, LANES - HEAD_DIM // 2, 1),
                            pltpu.roll(u, HEAD_DIM // 2, 1))
        return u * cos + partner * sin

    off = 3 * ca
    scale = HEAD_DIM ** -0.5 * math.log2(math.e)
    u_q = _dot(h, w_ref[:, off:off + diff])
    for hh in range(N_HEADS):
        q_h = rope(u_q[:, hh * LANES:(hh + 1) * LANES]) * scale
        qT_ref[0, hh] = q_h.T.astype(BF16)
    off += diff
    u_k = _dot(h, w_ref[:, off:off + diff])
    for hh in range(N_HEADS):
        k_ref[0, :, hh * LANES:(hh + 1) * LANES] = rope(u_k[:, hh * LANES:(hh + 1) * LANES]).astype(BF16)
    off += diff
    u_v = _dot(h, w_ref[:, off:off + diff])
    ones = jnp.ones((BF16_ROWS, tk), BF16)
    for hh in range(N_HEADS):
        v_hT = u_v[:, hh * V_DIM:(hh + 1) * V_DIM].T.astype(BF16)
        for t in range(tm // tk):
            vT_ref[0, hh, t, 0:V_DIM, :] = v_hT[:, t * tk:(t + 1) * tk]
            vT_ref[0, hh, t, V_DIM:V_ROWS, :] = ones
    off += diff
    u_c = _dot(h, w_ref[:, off:off + 2 * conf]) + glub_ref[...]
    glu_ref[0] = u_c[:, 0:conf] * jax.nn.sigmoid(u_c[:, conf:2 * conf])


def _proj_in(x, g, cos_t, sin_t, w_in, glu_b, *, tm, tk):
    b, s, d = x.shape
    ca, diff = d // 4, d // 2
    conf = d - ca - diff
    in_w = w_in.shape[1]
    nt = s // tm
    tok = lambda width: pl.BlockSpec((1, tm, width), lambda bi, i: (bi, i, 0))
    row = lambda width: pl.BlockSpec((1, width), lambda bi, i: (0, 0))
    return pl.pallas_call(
        functools.partial(_proj_in_kernel, widths=(ca, diff, conf), tk=tk),
        grid=(b, nt),
        in_specs=[tok(d), row(d), tok(LANES), tok(LANES),
                  pl.BlockSpec((d, in_w), lambda bi, i: (0, 0)), row(2 * conf)],
        out_specs=[tok(ca), tok(ca), tok(conf),
                   pl.BlockSpec((1, N_HEADS, 2 * HEAD_DIM, tm), lambda bi, i: (bi, 0, 0, i)),
                   tok(diff),
                   pl.BlockSpec((1, N_HEADS, tm // tk, V_ROWS, tk), lambda bi, i: (bi, 0, i, 0, 0))],
        out_shape=[jax.ShapeDtypeStruct((b, s, ca), F32),
                   jax.ShapeDtypeStruct((b, s, ca), F32),
                   jax.ShapeDtypeStruct((b, s, conf), F32),
                   jax.ShapeDtypeStruct((b, N_HEADS, 2 * HEAD_DIM, s), BF16),
                   jax.ShapeDtypeStruct((b, s, diff), BF16),
                   jax.ShapeDtypeStruct((b, N_HEADS, s // tk, V_ROWS, tk), BF16)],
        compiler_params=pltpu.CompilerParams(
            dimension_semantics=("arbitrary", "arbitrary"), vmem_limit_bytes=VMEM_LIMIT),
        name="proj_in",
    )(x, g, cos_t, sin_t, w_in, glu_b)


def _attn_kernel(lq1_ref, lk1_ref, lq2_ref, lk2_ref, g_ref, qT_ref, k_ref, vT_ref,
                 o_ref, q2T_ref, *state, tq, tk, lam_init):
    m_refs, acc_refs = state[:N_HEADS], state[N_HEADS:]
    i = pl.program_id(1)
    feat = lax.broadcasted_iota(jnp.int32, (2 * HEAD_DIM, tq), 0)
    for hh in range(N_HEADS):
        qT = qT_ref[0, hh]
        zero = jnp.zeros_like(qT)
        q2T_ref[hh, :, 0:tq] = jnp.where(feat < HEAD_DIM, qT, zero)
        q2T_ref[hh, :, tq:2 * tq] = jnp.where(feat >= HEAD_DIM, qT, zero)
        m_refs[hh][...] = jnp.full(m_refs[hh].shape, MASK_VALUE, F32)
        acc_refs[hh][...] = jnp.zeros(acc_refs[hh].shape, F32)

    def step(key0, nkeys, vT_of, diagonal):
        rows = pl.ds(pl.multiple_of(key0, nkeys), nkeys)
        if diagonal:
            key = lax.broadcasted_iota(jnp.int32, (nkeys, 2 * tq), 0)
            col = lax.broadcasted_iota(jnp.int32, (nkeys, 2 * tq), 1)
            keep = key <= jnp.where(col >= tq, col - tq, col)
        sTs = [_dot(k_ref[0, rows, hh * LANES:(hh + 1) * LANES], q2T_ref[hh])
               for hh in range(N_HEADS)]
        pTs, alphas = [], []
        for hh in range(N_HEADS):
            sT = jnp.where(keep, sTs[hh], MASK_VALUE) if diagonal else sTs[hh]
            m_old = m_refs[hh][...]
            m_new = jnp.maximum(m_old, jnp.max(sT, axis=0, keepdims=True))
            m_refs[hh][...] = m_new
            alphas.append(jnp.exp2(m_old - m_new))
            pTs.append(jnp.exp2(sT - m_new).astype(BF16))
        for hh in range(N_HEADS):
            acc_refs[hh][...] = alphas[hh] * acc_refs[hh][...] + _dot(vT_of(hh), pTs[hh])

    ratio = tk // tq
    n_full = i // ratio
    sub = i % ratio

    def body(j, carry):
        step(j * tk, tk, lambda hh: vT_ref[0, hh, j], diagonal=False)
        return carry

    lax.fori_loop(0, n_full, body, 0)
    for c in range(ratio):
        chunk = functools.partial(
            step, n_full * tk + c * tq, tq,
            lambda hh, c=c: vT_ref[0, hh, n_full, :, c * tq:(c + 1) * tq])
        if c < ratio - 1:
            pl.when(sub > c)(functools.partial(chunk, diagonal=False))
        pl.when(sub == c)(functools.partial(chunk, diagonal=True))

    lam = (jnp.exp(jnp.sum(lq1_ref[...] * lk1_ref[...], keepdims=True))
           - jnp.exp(jnp.sum(lq2_ref[...] * lk2_ref[...], keepdims=True)) + lam_init)
    for hh in range(N_HEADS):
        acc = acc_refs[hh][...]
        oT = acc[0:V_DIM, :] / acc[V_DIM:V_DIM + 1, :]
        o = (oT[:, 0:tq] - lam * oT[:, tq:2 * tq]).T
        o_ref[0, :, hh * V_DIM:(hh + 1) * V_DIM] = (
            _rms(o, g_ref[...]) * (1.0 - lam_init)).astype(o_ref.dtype)


def _attn(qT, k, vT, lq1, lk1, lq2, lk2, subln_g, *, tq, tk, lam_init):
    b, nh, _, s = qT.shape
    nk = s // tk
    row = lambda width: pl.BlockSpec((1, width), lambda bi, i: (0, 0))
    return pl.pallas_call(
        functools.partial(_attn_kernel, tq=tq, tk=tk, lam_init=lam_init),
        grid=(b, s // tq),
        in_specs=[row(HEAD_DIM), row(HEAD_DIM), row(HEAD_DIM), row(HEAD_DIM), row(V_DIM),
                  pl.BlockSpec((1, nh, 2 * HEAD_DIM, tq), lambda bi, i: (bi, 0, 0, i)),
                  pl.BlockSpec((1, s, nh * 2 * HEAD_DIM), lambda bi, i: (bi, 0, 0)),
                  pl.BlockSpec((1, nh, nk, V_ROWS, tk), lambda bi, i: (bi, 0, 0, 0, 0))],
        out_specs=pl.BlockSpec((1, tq, nh * V_DIM), lambda bi, i: (bi, i, 0)),
        out_shape=jax.ShapeDtypeStruct((b, s, nh * V_DIM), BF16),
        scratch_shapes=([pltpu.VMEM((nh, 2 * HEAD_DIM, 2 * tq), BF16)]
                        + [pltpu.VMEM((1, 2 * tq), F32)] * nh
                        + [pltpu.VMEM((V_ROWS, 2 * tq), F32)] * nh),
        compiler_params=pltpu.CompilerParams(
            dimension_semantics=("arbitrary", "arbitrary"), vmem_limit_bytes=VMEM_LIMIT),
        name="attn",
    )(lq1, lk1, lq2, lk2, subln_g, qT, k, vT)


def _causal_dwconv(win_ref, w_ref, halo, taps, tm, emit):
    base = halo - (taps - 1)
    for r0 in range(0, tm, CONV_ROWS):
        acc = w_ref[0:1, :] * win_ref[pl.ds(base + r0, CONV_ROWS), :]
        for j in range(1, taps):
            acc = acc + w_ref[j:j + 1, :] * win_ref[pl.ds(base + r0 + j, CONV_ROWS), :]
        emit(r0, CONV_ROWS, acc)


def _mix_out_kernel(x_ref, ab_ref, acx_ref, acx_halo_ref, glu_ref, glu_halo_ref, yb_ref,
                    scw_ref, dww_ref, dwb_ref, lng_ref, lnb_ref, wout_ref,
                    o_ref, wina_ref, winc_ref, y_ref, *, widths):
    ca, diff, conf = widths
    tm = x_ref.shape[1]
    first = pl.program_id(1) == 0

    wina_ref[0:SHORT_HALO, :] = jnp.where(first, 0.0, acx_halo_ref[0])
    wina_ref[SHORT_HALO:SHORT_HALO + tm, :] = acx_ref[0]
    winc_ref[0:CONF_HALO, :] = jnp.where(first, 0.0, glu_halo_ref[0])
    winc_ref[CONF_HALO:CONF_HALO + tm, :] = glu_ref[0]

    def emit_a(r0, rows, conv):
        y_ref[pl.ds(r0, rows), 0:ca] = (ab_ref[0, pl.ds(r0, rows), :] * conv).astype(BF16)

    _causal_dwconv(wina_ref, scw_ref, SHORT_HALO, SHORT_K, tm, emit_a)

    y_ref[:, ca:ca + diff] = yb_ref[0]

    def emit_c(r0, rows, conv):
        c = conv + dwb_ref[...]
        mu = jnp.mean(c, axis=-1, keepdims=True)
        cc = c - mu
        ln = cc * lax.rsqrt(jnp.mean(cc * cc, axis=-1, keepdims=True) + LN_EPS)
        ln = ln * lng_ref[...] + lnb_ref[...]
        y_ref[pl.ds(r0, rows), ca + diff:ca + diff + conf] = (ln * jax.nn.sigmoid(ln)).astype(BF16)

    _causal_dwconv(winc_ref, dww_ref, CONF_HALO, CONF_K, tm, emit_c)

    o_ref[0] = x_ref[0] + _dot(y_ref[...], wout_ref[...])


def _mix_out(x, ab, acx, glu, yb, scw, dww, dwb, lng, lnb, w_out, *, tm):
    b, s, d = x.shape
    ca, diff = d // 4, d // 2
    conf = d - ca - diff
    tok = lambda width: pl.BlockSpec((1, tm, width), lambda bi, i: (bi, i, 0))
    halo = lambda rows, width: pl.BlockSpec(
        (1, rows, width), lambda bi, i: (bi, jnp.maximum(i * (tm // rows) - 1, 0), 0))
    full = lambda r, c: pl.BlockSpec((r, c), lambda bi, i: (0, 0))
    return pl.pallas_call(
        functools.partial(_mix_out_kernel, widths=(ca, diff, conf)),
        grid=(b, s // tm),
        in_specs=[tok(d), tok(ca), tok(ca), halo(SHORT_HALO, ca), tok(conf), halo(CONF_HALO, conf),
                  tok(diff), full(SHORT_K, ca), full(CONF_K, conf), full(1, conf), full(1, conf),
                  full(1, conf), full(d, d)],
        out_specs=tok(d),
        out_shape=jax.ShapeDtypeStruct((b, s, d), F32),
        scratch_shapes=[pltpu.VMEM((SHORT_HALO + tm, ca), F32),
                        pltpu.VMEM((CONF_HALO + tm, conf), F32),
                        pltpu.VMEM((tm, d), BF16)],
        compiler_params=pltpu.CompilerParams(
            dimension_semantics=("arbitrary", "arbitrary"), vmem_limit_bytes=VMEM_LIMIT),
        name="mix_out",
    )(x, ab, acx, acx, glu, glu, yb, scw, dww, dwb, lng, lnb, w_out)


def _ffn_kernel(x_ref, g_ref, wg_ref, wu_ref, wd_ref, fg_ref, o_ref, act_ref, *, final):
    x = x_ref[...]
    h = _rms(x, g_ref[...]).astype(BF16)
    hidden = wg_ref.shape[1]
    for c0 in range(0, hidden, FFN_CHUNK):
        gate = _dot(h, wg_ref[:, c0:c0 + FFN_CHUNK])
        up = _dot(h, wu_ref[:, c0:c0 + FFN_CHUNK])
        act_ref[:, c0:c0 + FFN_CHUNK] = (gate * jax.nn.sigmoid(gate) * up).astype(BF16)
    y = x + _dot(act_ref[...], wd_ref[...])
    if final:
        y = _rms(y, fg_ref[...])
    o_ref[...] = y


def _ffn(x2d, g, w_gate, w_up, w_down, final_g, *, tm, final):
    t, d = x2d.shape
    hidden = w_gate.shape[1]
    tok = pl.BlockSpec((tm, d), lambda i: (i, 0))
    const = lambda r, c: pl.BlockSpec((r, c), lambda i: (0, 0), pipeline_mode=pl.Buffered(1))
    return pl.pallas_call(
        functools.partial(_ffn_kernel, final=final),
        grid=(t // tm,),
        in_specs=[tok, const(1, d), const(d, hidden), const(d, hidden), const(hidden, d), const(1, d)],
        out_specs=tok,
        out_shape=jax.ShapeDtypeStruct((t, d), F32),
        scratch_shapes=[pltpu.VMEM((tm, hidden), BF16)],
        compiler_params=pltpu.CompilerParams(
            dimension_semantics=("arbitrary",), vmem_limit_bytes=VMEM_LIMIT),
        name="ffn",
    )(x2d, g, w_gate, w_up, w_down, final_g)


def _rope_tables(positions):
    half = HEAD_DIM // 2
    inv_freq = 1.0 / (ROPE_THETA ** (jnp.arange(0, HEAD_DIM, 2, dtype=F32) / HEAD_DIM))
    ang = positions.astype(F32)[..., None] * inv_freq
    cos, sin = jnp.cos(ang), jnp.sin(ang)
    reps = LANES // half
    cos_t = jnp.tile(cos, (1, 1, reps))
    sin_t = jnp.tile(jnp.concatenate([-sin, sin], axis=-1), (1, 1, reps // 2))
    return cos_t, sin_t


def _head_major(w, d, ca, diff):
    def reorder(cols):
        return cols.reshape(d, 2, N_HEADS, HEAD_DIM).transpose(0, 2, 1, 3).reshape(d, diff)
    q0 = 3 * ca
    return jnp.concatenate(
        [w[:, :q0], reorder(w[:, q0:q0 + diff]), reorder(w[:, q0 + diff:q0 + 2 * diff]),
         w[:, q0 + 2 * diff:]], axis=1)


def kernel(x, positions, mix_norm_g, w_in, short_conv_w, glu_b, conf_dw_w, conf_dw_b, conf_ln_g, conf_ln_b, lam_q1, lam_k1, lam_q2, lam_k2, diff_subln_g, w_out, ffn_norm_g, w_gate, w_up, w_down, final_norm_g):
    b, s, d = x.shape
    depth = w_in.shape[0]
    ca, diff = d // 4, d // 2
    assert diff == 2 * N_HEADS * HEAD_DIM and N_HEADS * V_DIM == diff
    tm = min(TM_PROJ, s)
    tq, tk = min(TQ, s), min(TK, s)
    assert s % tm == 0 and s % tk == 0 and tk % tq == 0 and tm % tk == 0

    cos_t, sin_t = _rope_tables(positions)
    row = lambda v: v.reshape(1, -1)
    for l in range(depth):
        lam_init = 0.8 - 0.6 * math.exp(-0.3 * l)
        w_in_l = _head_major(w_in[l], d, ca, diff).astype(BF16)
        ab, acx, glu, qT, k, vT = _proj_in(x, row(mix_norm_g[l]), cos_t, sin_t, w_in_l,
                                          row(glu_b[l]), tm=tm, tk=tk)
        yb = _attn(qT, k, vT, row(lam_q1[l]), row(lam_k1[l]), row(lam_q2[l]), row(lam_k2[l]),
                   row(diff_subln_g[l]), tq=tq, tk=tk, lam_init=lam_init)
        x = _mix_out(x, ab, acx, glu, yb, short_conv_w[l], conf_dw_w[l], row(conf_dw_b[l]),
                     row(conf_ln_g[l]), row(conf_ln_b[l]), w_out[l].astype(BF16), tm=tm)
        x = _ffn(x.reshape(b * s, d), row(ffn_norm_g[l]), w_gate[l].astype(BF16),
                 w_up[l].astype(BF16), w_down[l].astype(BF16), row(final_norm_g),
                 tm=tm, final=(l == depth - 1)).reshape(b, s, d)
    return x
```

```python
import functools
import math

import jax
import jax.numpy as jnp
from jax import lax
from jax.experimental import pallas as pl
from jax.experimental.pallas import tpu as pltpu

F32 = jnp.float32
BF16 = jnp.bfloat16

N_HEADS = 4
HEAD_DIM = 64
V_DIM = 2 * HEAD_DIM
SHORT_K = 3
CONF_K = 31
ROPE_THETA = 10000.0
RMS_EPS = 1e-6
LN_EPS = 1e-5

LANES = 128
SUBLANES = 8
BF16_ROWS = 16
VMEM_LIMIT = 56 * 1024 * 1024

TM_PROJ = 512
TQ = 256
TK = 512
CONV_ROWS = 32
COPY_ROWS = 128
FFN_CHUNK = 256
SHORT_HALO = SUBLANES
CONF_HALO = 32
V_ROWS = V_DIM + BF16_ROWS
MASK_VALUE = -1e30


def _rms(x, g):
    return x * lax.rsqrt(jnp.mean(x * x, axis=-1, keepdims=True) + RMS_EPS) * g


def _dot(a, b):
    return jnp.dot(a, b, preferred_element_type=F32)


def _conv_shifts(halo, taps):
    shifts = {(halo - (taps - 1) + j) % SUBLANES for j in range(taps)}
    return [0] + sorted(shifts - {0})


def _shift_planes(win_ref, halo, taps, tm):
    shifts = _conv_shifts(halo, taps)
    offsets = [halo - (taps - 1) + j for j in range(taps)]
    for p, sh in enumerate(shifts[1:], start=1):
        n = tm + max(o - sh for o in offsets if o % SUBLANES == sh)
        for r0 in range(0, n, COPY_ROWS):
            rows = min(COPY_ROWS, n - r0)
            ext = win_ref[0, pl.ds(r0, rows + SUBLANES), :]
            win_ref[p, pl.ds(r0, rows), :] = pltpu.roll(ext, rows + SUBLANES - sh, 0)[0:rows]


def _conv_rows(win_ref, w_ref, halo, taps, r0):
    shifts = _conv_shifts(halo, taps)
    groups = CONV_ROWS // SUBLANES
    accs = [None] * groups
    for j in range(taps):
        o = halo - (taps - 1) + j
        sh = o % SUBLANES
        w_j = w_ref[j]
        for gi in range(groups):
            term = w_j * win_ref[shifts.index(sh), pl.ds(r0 + o - sh + gi * SUBLANES, SUBLANES), :]
            accs[gi] = term if accs[gi] is None else accs[gi] + term
    return jnp.concatenate(accs, axis=0)


def _proj_in_kernel(x_ref, g_ref, cos_ref, sin_ref, w_ref, glub_ref,
                    scw_ref, dww_ref, dwb_ref, lng_ref, lnb_ref,
                    ya_ref, yc_ref, qT_ref, k_ref, vT_ref,
                    ab_ref, u_ref, wina_ref, winc_ref, *, widths, tk):
    ca, diff, conf = widths
    tm = x_ref.shape[1]
    h = _rms(x_ref[0], g_ref[...]).astype(BF16)

    @pl.when(pl.program_id(1) == 0)
    def _():
        wina_ref[0, 0:SHORT_HALO, :] = jnp.zeros((SHORT_HALO, ca), F32)
        winc_ref[0, 0:CONF_HALO, :] = jnp.zeros((CONF_HALO, conf), F32)

    c_off = 3 * ca + 3 * diff
    u_c = _dot(h, w_ref[:, c_off:c_off + 2 * conf]) + glub_ref[...]
    winc_ref[0, CONF_HALO:CONF_HALO + tm, :] = u_c[:, 0:conf] * jax.nn.sigmoid(u_c[:, conf:2 * conf])
    _shift_planes(winc_ref, CONF_HALO, CONF_K, tm)

    def conformer_rows(r0):
        c = _conv_rows(winc_ref, dww_ref, CONF_HALO, CONF_K, r0) + dwb_ref[...]
        mu = jnp.mean(c, axis=-1, keepdims=True)
        cc = c - mu
        ln = cc * lax.rsqrt(jnp.mean(cc * cc, axis=-1, keepdims=True) + LN_EPS)
        ln = ln * lng_ref[...] + lnb_ref[...]
        yc_ref[0, pl.ds(r0, CONV_ROWS), :] = (ln * jax.nn.sigmoid(ln)).astype(BF16)

    def keep_gate(u_a):
        ab_ref[...] = u_a[:, 0:ca]
        wina_ref[0, SHORT_HALO:SHORT_HALO + tm, :] = u_a[:, ca:2 * ca] * u_a[:, 2 * ca:3 * ca]

    def keep_qkv(seg, u):
        u_ref[:, seg * diff:(seg + 1) * diff] = u

    keep_gate(_dot(h, w_ref[:, 0:3 * ca]))
    for seg in range(3):
        keep_qkv(seg, _dot(h, w_ref[:, 3 * ca + seg * diff:3 * ca + (seg + 1) * diff]))
    row_starts = list(range(0, tm, CONV_ROWS))
    for r0 in row_starts:
        conformer_rows(r0)

    _shift_planes(wina_ref, SHORT_HALO, SHORT_K, tm)
    for r0 in row_starts:
        conv = _conv_rows(wina_ref, scw_ref, SHORT_HALO, SHORT_K, r0)
        ya_ref[0, pl.ds(r0, CONV_ROWS), :] = (ab_ref[pl.ds(r0, CONV_ROWS), :] * conv).astype(BF16)
    wina_ref[0, 0:SHORT_HALO, :] = wina_ref[0, tm:tm + SHORT_HALO, :]
    winc_ref[0, 0:CONF_HALO, :] = winc_ref[0, tm:tm + CONF_HALO, :]

    cos = cos_ref[0]
    sin = sin_ref[0]
    lane = lax.broadcasted_iota(jnp.int32, (tm, LANES), 1)
    lower_half = (lane % HEAD_DIM) < (HEAD_DIM // 2)

    def rope(u):
        partner = jnp.where(lower_half, pltpu.roll(u, LANES - HEAD_DIM // 2, 1),
                            pltpu.roll(u, HEAD_DIM // 2, 1))
        return u * cos + partner * sin

    scale = HEAD_DIM ** -0.5 * math.log2(math.e)
    ones = jnp.ones((BF16_ROWS, tk), BF16)
    for hh in range(N_HEADS):
        q_h = rope(u_ref[:, hh * LANES:(hh + 1) * LANES]) * scale
        qT_ref[0, hh] = q_h.T.astype(BF16)
        k_h = rope(u_ref[:, diff + hh * LANES:diff + (hh + 1) * LANES])
        k_ref[0, :, hh * LANES:(hh + 1) * LANES] = k_h.astype(BF16)
        v_hT = u_ref[:, 2 * diff + hh * V_DIM:2 * diff + (hh + 1) * V_DIM].T.astype(BF16)
        for t in range(tm // tk):
            vT_ref[0, hh, t, 0:V_DIM, :] = v_hT[:, t * tk:(t + 1) * tk]
            vT_ref[0, hh, t, V_DIM:V_ROWS, :] = ones


def _proj_in(x, g, cos_t, sin_t, w_in, glu_b, scw, dww, dwb, lng, lnb, *, tm, tk):
    b, s, d = x.shape
    ca, diff = d // 4, d // 2
    conf = d - ca - diff
    in_w = w_in.shape[1]
    nt = s // tm
    tok = lambda width: pl.BlockSpec((1, tm, width), lambda bi, i: (bi, i, 0))
    row = lambda width: pl.BlockSpec((1, width), lambda bi, i: (0, 0))
    taps = lambda k, c: pl.BlockSpec((k, SUBLANES, c), lambda bi, i: (0, 0, 0))
    sublane_tiled = lambda w: jnp.broadcast_to(w[:, None, :], (w.shape[0], SUBLANES, w.shape[1]))
    return pl.pallas_call(
        functools.partial(_proj_in_kernel, widths=(ca, diff, conf), tk=tk),
        grid=(b, nt),
        in_specs=[tok(d), row(d), tok(LANES), tok(LANES),
                  pl.BlockSpec((d, in_w), lambda bi, i: (0, 0)), row(2 * conf),
                  taps(SHORT_K, ca), taps(CONF_K, conf), row(conf), row(conf), row(conf)],
        out_specs=[tok(ca), tok(conf),
                   pl.BlockSpec((1, N_HEADS, 2 * HEAD_DIM, tm), lambda bi, i: (bi, 0, 0, i)),
                   tok(diff),
                   pl.BlockSpec((1, N_HEADS, tm // tk, V_ROWS, tk), lambda bi, i: (bi, 0, i, 0, 0))],
        out_shape=[jax.ShapeDtypeStruct((b, s, ca), BF16),
                   jax.ShapeDtypeStruct((b, s, conf), BF16),
                   jax.ShapeDtypeStruct((b, N_HEADS, 2 * HEAD_DIM, s), BF16),
                   jax.ShapeDtypeStruct((b, s, diff), BF16),
                   jax.ShapeDtypeStruct((b, N_HEADS, s // tk, V_ROWS, tk), BF16)],
        scratch_shapes=[pltpu.VMEM((tm, ca), F32), pltpu.VMEM((tm, 3 * diff), F32),
                        pltpu.VMEM((len(_conv_shifts(SHORT_HALO, SHORT_K)), SHORT_HALO + tm, ca), F32),
                        pltpu.VMEM((len(_conv_shifts(CONF_HALO, CONF_K)), CONF_HALO + tm, conf), F32)],
        compiler_params=pltpu.CompilerParams(
            dimension_semantics=("arbitrary", "arbitrary"), vmem_limit_bytes=VMEM_LIMIT),
        name="proj_in",
    )(x, g, cos_t, sin_t, w_in, glu_b, sublane_tiled(scw), sublane_tiled(dww), dwb, lng, lnb)


def _attn_kernel(lq1_ref, lk1_ref, lq2_ref, lk2_ref, g_ref, qT_ref, k_ref, vT_ref,
                 o_ref, q2T_ref, *state, tq, tk, lam_init):
    m_refs, acc_refs = state[:N_HEADS], state[N_HEADS:2 * N_HEADS]
    s_bufs = (state[2 * N_HEADS:3 * N_HEADS], state[3 * N_HEADS:4 * N_HEADS])
    i = pl.program_id(1)
    feat = lax.broadcasted_iota(jnp.int32, (2 * HEAD_DIM, tq), 0)
    for hh in range(N_HEADS):
        qT = qT_ref[0, hh]
        zero = jnp.zeros_like(qT)
        q2T_ref[hh, :, 0:tq] = jnp.where(feat < HEAD_DIM, qT, zero)
        q2T_ref[hh, :, tq:2 * tq] = jnp.where(feat >= HEAD_DIM, qT, zero)
        m_refs[hh][...] = jnp.full(m_refs[hh].shape, MASK_VALUE, F32)
        acc_refs[hh][...] = jnp.zeros(acc_refs[hh].shape, F32)

    def scores(j):
        rows = pl.ds(pl.multiple_of(j * tk, tk), tk)
        return [_dot(k_ref[0, rows, hh * LANES:(hh + 1) * LANES], q2T_ref[hh])
                for hh in range(N_HEADS)]

    def update(sT_of, vT_of, keep):
        pTs, alphas = [], []
        for hh in range(N_HEADS):
            sT = sT_of(hh)
            if keep is not None:
                sT = jnp.where(keep, sT, MASK_VALUE)
            m_old = m_refs[hh][...]
            m_new = jnp.maximum(m_old, jnp.max(sT, axis=0, keepdims=True))
            m_refs[hh][...] = m_new
            alphas.append(jnp.exp2(m_old - m_new))
            pTs.append(jnp.exp2(sT - m_new).astype(BF16))
        for hh in range(N_HEADS):
            acc_refs[hh][...] = alphas[hh] * acc_refs[hh][...] + _dot(vT_of(hh), pTs[hh])

    def pipelined(j, cur, nxt):
        s_next = scores(j + 1)
        update(lambda hh: cur[hh][...], lambda hh: vT_ref[0, hh, j], None)
        for hh in range(N_HEADS):
            nxt[hh][...] = s_next[hh]

    def diagonal_tile(buf, c, j):
        nkeys = (c + 1) * tq
        key = lax.broadcasted_iota(jnp.int32, (nkeys, 2 * tq), 0)
        col = lax.broadcasted_iota(jnp.int32, (nkeys, 2 * tq), 1)
        keep = key <= jnp.where(col >= tq, col - tq, col) + c * tq
        update(lambda hh: buf[hh][0:nkeys, :], lambda hh: vT_ref[0, hh, j, :, 0:nkeys], keep)

    ratio = tk // tq
    n_full = i // ratio
    sub = i % ratio
    odd = n_full % 2

    first = scores(0)
    for hh in range(N_HEADS):
        s_bufs[0][hh][...] = first[hh]

    def pair(p, carry):
        pipelined(2 * p, s_bufs[0], s_bufs[1])
        pipelined(2 * p + 1, s_bufs[1], s_bufs[0])
        return carry

    lax.fori_loop(0, n_full // 2, pair, 0)
    pl.when(odd == 1)(lambda: pipelined(n_full - 1, s_bufs[0], s_bufs[1]))
    for parity in range(2):
        for c in range(ratio):
            pl.when(jnp.logical_and(odd == parity, sub == c))(
                functools.partial(diagonal_tile, s_bufs[parity], c, n_full))

    lam = (jnp.exp(jnp.sum(lq1_ref[...] * lk1_ref[...], keepdims=True))
           - jnp.exp(jnp.sum(lq2_ref[...] * lk2_ref[...], keepdims=True)) + lam_init)
    for hh in range(N_HEADS):
        acc = acc_refs[hh][...]
        oT = acc[0:V_DIM, :] / acc[V_DIM:V_DIM + 1, :]
        o = (oT[:, 0:tq] - lam * oT[:, tq:2 * tq]).T
        o_ref[0, :, hh * V_DIM:(hh + 1) * V_DIM] = (
            _rms(o, g_ref[...]) * (1.0 - lam_init)).astype(o_ref.dtype)


def _attn(qT, k, vT, lq1, lk1, lq2, lk2, subln_g, *, tq, tk, lam_init):
    b, nh, _, s = qT.shape
    nk = s // tk
    row = lambda width: pl.BlockSpec((1, width), lambda bi, i: (0, 0))
    return pl.pallas_call(
        functools.partial(_attn_kernel, tq=tq, tk=tk, lam_init=lam_init),
        grid=(b, s // tq),
        in_specs=[row(HEAD_DIM), row(HEAD_DIM), row(HEAD_DIM), row(HEAD_DIM), row(V_DIM),
                  pl.BlockSpec((1, nh, 2 * HEAD_DIM, tq), lambda bi, i: (bi, 0, 0, i)),
                  pl.BlockSpec((1, s, nh * 2 * HEAD_DIM), lambda bi, i: (bi, 0, 0)),
                  pl.BlockSpec((1, nh, nk, V_ROWS, tk), lambda bi, i: (bi, 0, 0, 0, 0))],
        out_specs=pl.BlockSpec((1, tq, nh * V_DIM), lambda bi, i: (bi, i, 0)),
        out_shape=jax.ShapeDtypeStruct((b, s, nh * V_DIM), BF16),
        scratch_shapes=([pltpu.VMEM((nh, 2 * HEAD_DIM, 2 * tq), BF16)]
                        + [pltpu.VMEM((1, 2 * tq), F32)] * nh
                        + [pltpu.VMEM((V_ROWS, 2 * tq), F32)] * nh
                        + [pltpu.VMEM((tk, 2 * tq), F32)] * (2 * nh)),
        compiler_params=pltpu.CompilerParams(
            dimension_semantics=("arbitrary", "arbitrary"), vmem_limit_bytes=VMEM_LIMIT),
        name="attn",
    )(lq1, lk1, lq2, lk2, subln_g, qT, k, vT)


def _ffn_kernel(x_ref, ya_ref, yb_ref, yc_ref, wout_ref, g_ref, wg_ref, wu_ref, wd_ref, fg_ref,
                o_ref, act_ref, *, final):
    ca, diff = ya_ref.shape[1], yb_ref.shape[1]
    x = (x_ref[...] + _dot(ya_ref[...], wout_ref[0:ca, :])
         + _dot(yb_ref[...], wout_ref[ca:ca + diff, :])
         + _dot(yc_ref[...], wout_ref[ca + diff:, :]))
    h = _rms(x, g_ref[...]).astype(BF16)
    hidden = wg_ref.shape[1]
    for c0 in range(0, hidden, FFN_CHUNK):
        gate = _dot(h, wg_ref[:, c0:c0 + FFN_CHUNK])
        up = _dot(h, wu_ref[:, c0:c0 + FFN_CHUNK])
        act_ref[:, c0:c0 + FFN_CHUNK] = (gate * jax.nn.sigmoid(gate) * up).astype(BF16)
    y = x + _dot(act_ref[...], wd_ref[...])
    if final:
        y = _rms(y, fg_ref[...])
    o_ref[...] = y


def _ffn(x2d, ya, yb, yc, w_out, g, w_gate, w_up, w_down, final_g, *, tm, final):
    t, d = x2d.shape
    hidden = w_gate.shape[1]
    tok = lambda width: pl.BlockSpec((tm, width), lambda i: (i, 0))
    const = lambda r, c: pl.BlockSpec((r, c), lambda i: (0, 0), pipeline_mode=pl.Buffered(1))
    return pl.pallas_call(
        functools.partial(_ffn_kernel, final=final),
        grid=(t // tm,),
        in_specs=[tok(d), tok(ya.shape[1]), tok(yb.shape[1]), tok(yc.shape[1]), const(d, d),
                  const(1, d), const(d, hidden), const(d, hidden), const(hidden, d), const(1, d)],
        out_specs=tok(d),
        out_shape=jax.ShapeDtypeStruct((t, d), F32),
        scratch_shapes=[pltpu.VMEM((tm, hidden), BF16)],
        compiler_params=pltpu.CompilerParams(
            dimension_semantics=("arbitrary",), vmem_limit_bytes=VMEM_LIMIT),
        name="ffn",
    )(x2d, ya, yb, yc, w_out, g, w_gate, w_up, w_down, final_g)


def _rope_tables(positions):
    half = HEAD_DIM // 2
    inv_freq = 1.0 / (ROPE_THETA ** (jnp.arange(0, HEAD_DIM, 2, dtype=F32) / HEAD_DIM))
    ang = positions.astype(F32)[..., None] * inv_freq
    cos, sin = jnp.cos(ang), jnp.sin(ang)
    reps = LANES // half
    cos_t = jnp.tile(cos, (1, 1, reps))
    sin_t = jnp.tile(jnp.concatenate([-sin, sin], axis=-1), (1, 1, reps // 2))
    return cos_t, sin_t


def _head_major(w, d, ca, diff):
    def reorder(cols):
        return cols.reshape(d, 2, N_HEADS, HEAD_DIM).transpose(0, 2, 1, 3).reshape(d, diff)
    q0 = 3 * ca
    return jnp.concatenate(
        [w[:, :q0], reorder(w[:, q0:q0 + diff]), reorder(w[:, q0 + diff:q0 + 2 * diff]),
         w[:, q0 + 2 * diff:]], axis=1)


def kernel(x, positions, mix_norm_g, w_in, short_conv_w, glu_b, conf_dw_w, conf_dw_b, conf_ln_g, conf_ln_b, lam_q1, lam_k1, lam_q2, lam_k2, diff_subln_g, w_out, ffn_norm_g, w_gate, w_up, w_down, final_norm_g):
    b, s, d = x.shape
    depth = w_in.shape[0]
    ca, diff = d // 4, d // 2
    assert diff == 2 * N_HEADS * HEAD_DIM and N_HEADS * V_DIM == diff
    tm = min(TM_PROJ, s)
    tq, tk = min(TQ, s), min(TK, s)
    assert s % tm == 0 and s % tk == 0 and tk % tq == 0 and tm % tk == 0

    cos_t, sin_t = _rope_tables(positions)
    row = lambda v: v.reshape(1, -1)
    for l in range(depth):
        lam_init = 0.8 - 0.6 * math.exp(-0.3 * l)
        w_in_l = _head_major(w_in[l], d, ca, diff).astype(BF16)
        ya, yc, qT, k, vT = _proj_in(
            x, row(mix_norm_g[l]), cos_t, sin_t, w_in_l, row(glu_b[l]), short_conv_w[l], conf_dw_w[l],
            row(conf_dw_b[l]), row(conf_ln_g[l]), row(conf_ln_b[l]), tm=tm, tk=tk)
        yb = _attn(qT, k, vT, row(lam_q1[l]), row(lam_k1[l]), row(lam_q2[l]), row(lam_k2[l]),
                   row(diff_subln_g[l]), tq=tq, tk=tk, lam_init=lam_init)
        flat = lambda a: a.reshape(b * s, a.shape[-1])
        x = _ffn(flat(x), flat(ya), flat(yb), flat(yc), w_out[l].astype(BF16), row(ffn_norm_g[l]),
                 w_gate[l].astype(BF16), w_up[l].astype(BF16), w_down[l].astype(BF16),
                 row(final_norm_g), tm=tm, final=(l == depth - 1)).reshape(b, s, d)
    return x
```

```python
import functools
import math

import jax
import jax.numpy as jnp
from jax import lax
from jax.experimental import pallas as pl
from jax.experimental.pallas import tpu as pltpu

F32 = jnp.float32
BF16 = jnp.bfloat16

N_HEADS = 4
HEAD_DIM = 64
V_DIM = 2 * HEAD_DIM
SHORT_K = 3
CONF_K = 31
ROPE_THETA = 10000.0
RMS_EPS = 1e-6
LN_EPS = 1e-5

LANES = 128
SUBLANES = 8
BF16_ROWS = 16
VMEM_LIMIT = 56 * 1024 * 1024

TM_PROJ = 512
TQ = 256
TK = 512
CONV_ROWS = 32
COPY_ROWS = 128
FFN_CHUNK = 256
SHORT_HALO = SUBLANES
CONF_HALO = 32
V_ROWS = V_DIM + BF16_ROWS
MASK_VALUE = -1e30
SHIFT_MARGIN = 1.0 + 2.0 ** -7
MIN_ROW_SUM = 2.0 ** -80


def _rms(x, g):
    return x * lax.rsqrt(jnp.mean(x * x, axis=-1, keepdims=True) + RMS_EPS) * g


def _dot(a, b):
    return jnp.dot(a, b, preferred_element_type=F32)


def _conv_shifts(halo, taps):
    shifts = {(halo - (taps - 1) + j) % SUBLANES for j in range(taps)}
    return [0] + sorted(shifts - {0})


def _shift_planes(win_ref, halo, taps, tm):
    shifts = _conv_shifts(halo, taps)
    offsets = [halo - (taps - 1) + j for j in range(taps)]
    for p, sh in enumerate(shifts[1:], start=1):
        n = tm + max(o - sh for o in offsets if o % SUBLANES == sh)
        for r0 in range(0, n, COPY_ROWS):
            rows = min(COPY_ROWS, n - r0)
            ext = win_ref[0, pl.ds(r0, rows + SUBLANES), :]
            win_ref[p, pl.ds(r0, rows), :] = pltpu.roll(ext, rows + SUBLANES - sh, 0)[0:rows]


def _conv_rows(win_ref, w_ref, halo, taps, r0):
    shifts = _conv_shifts(halo, taps)
    groups = CONV_ROWS // SUBLANES
    accs = [None] * groups
    for j in range(taps):
        o = halo - (taps - 1) + j
        sh = o % SUBLANES
        w_j = w_ref[j]
        for gi in range(groups):
            term = w_j * win_ref[shifts.index(sh), pl.ds(r0 + o - sh + gi * SUBLANES, SUBLANES), :]
            accs[gi] = term if accs[gi] is None else accs[gi] + term
    return jnp.concatenate(accs, axis=0)


def _proj_in_kernel(x_ref, g_ref, cos_ref, sin_ref, w_ref, glub_ref,
                    scw_ref, dww_ref, dwb_ref, lng_ref, lnb_ref,
                    ya_ref, yc_ref, qT_ref, k_ref, vT_ref,
                    ab_ref, u_ref, wina_ref, winc_ref, *, widths, tk):
    ca, diff, conf = widths
    tm = x_ref.shape[1]
    h = _rms(x_ref[0], g_ref[...]).astype(BF16)

    @pl.when(pl.program_id(1) == 0)
    def _():
        wina_ref[0, 0:SHORT_HALO, :] = jnp.zeros((SHORT_HALO, ca), F32)
        winc_ref[0, 0:CONF_HALO, :] = jnp.zeros((CONF_HALO, conf), F32)

    c_off = 3 * ca + 3 * diff
    u_c = _dot(h, w_ref[:, c_off:c_off + 2 * conf]) + glub_ref[...]
    winc_ref[0, CONF_HALO:CONF_HALO + tm, :] = u_c[:, 0:conf] * jax.nn.sigmoid(u_c[:, conf:2 * conf])
    _shift_planes(winc_ref, CONF_HALO, CONF_K, tm)

    def conformer_rows(r0):
        c = _conv_rows(winc_ref, dww_ref, CONF_HALO, CONF_K, r0) + dwb_ref[...]
        mu = jnp.mean(c, axis=-1, keepdims=True)
        cc = c - mu
        ln = cc * lax.rsqrt(jnp.mean(cc * cc, axis=-1, keepdims=True) + LN_EPS)
        ln = ln * lng_ref[...] + lnb_ref[...]
        yc_ref[0, pl.ds(r0, CONV_ROWS), :] = (ln * jax.nn.sigmoid(ln)).astype(BF16)

    def keep_gate(u_a):
        ab_ref[...] = u_a[:, 0:ca]
        wina_ref[0, SHORT_HALO:SHORT_HALO + tm, :] = u_a[:, ca:2 * ca] * u_a[:, 2 * ca:3 * ca]

    def keep_qkv(seg, u):
        u_ref[:, seg * diff:(seg + 1) * diff] = u

    keep_gate(_dot(h, w_ref[:, 0:3 * ca]))
    for seg in range(3):
        keep_qkv(seg, _dot(h, w_ref[:, 3 * ca + seg * diff:3 * ca + (seg + 1) * diff]))
    row_starts = list(range(0, tm, CONV_ROWS))
    for r0 in row_starts:
        conformer_rows(r0)

    _shift_planes(wina_ref, SHORT_HALO, SHORT_K, tm)
    for r0 in row_starts:
        conv = _conv_rows(wina_ref, scw_ref, SHORT_HALO, SHORT_K, r0)
        ya_ref[0, pl.ds(r0, CONV_ROWS), :] = (ab_ref[pl.ds(r0, CONV_ROWS), :] * conv).astype(BF16)
    wina_ref[0, 0:SHORT_HALO, :] = wina_ref[0, tm:tm + SHORT_HALO, :]
    winc_ref[0, 0:CONF_HALO, :] = winc_ref[0, tm:tm + CONF_HALO, :]

    cos = cos_ref[0]
    sin = sin_ref[0]
    lane = lax.broadcasted_iota(jnp.int32, (tm, LANES), 1)
    lower_half = (lane % HEAD_DIM) < (HEAD_DIM // 2)

    def rope(u):
        partner = jnp.where(lower_half, pltpu.roll(u, LANES - HEAD_DIM // 2, 1),
                            pltpu.roll(u, HEAD_DIM // 2, 1))
        return u * cos + partner * sin

    scale = HEAD_DIM ** -0.5 * math.log2(math.e)
    ones = jnp.ones((BF16_ROWS, tk), BF16)
    for hh in range(N_HEADS):
        q_h = rope(u_ref[:, hh * LANES:(hh + 1) * LANES]) * scale
        qT_ref[0, hh] = q_h.T.astype(BF16)
        k_h = rope(u_ref[:, diff + hh * LANES:diff + (hh + 1) * LANES])
        k_ref[0, :, hh * LANES:(hh + 1) * LANES] = k_h.astype(BF16)
        v_hT = u_ref[:, 2 * diff + hh * V_DIM:2 * diff + (hh + 1) * V_DIM].T.astype(BF16)
        for t in range(tm // tk):
            vT_ref[0, hh, t, 0:V_DIM, :] = v_hT[:, t * tk:(t + 1) * tk]
            vT_ref[0, hh, t, V_DIM:V_ROWS, :] = ones


def _proj_in(x, g, cos_t, sin_t, w_in, glu_b, scw, dww, dwb, lng, lnb, *, tm, tk):
    b, s, d = x.shape
    ca, diff = d // 4, d // 2
    conf = d - ca - diff
    in_w = w_in.shape[1]
    nt = s // tm
    tok = lambda width: pl.BlockSpec((1, tm, width), lambda bi, i: (bi, i, 0))
    row = lambda width: pl.BlockSpec((1, width), lambda bi, i: (0, 0))
    taps = lambda k, c: pl.BlockSpec((k, SUBLANES, c), lambda bi, i: (0, 0, 0))
    sublane_tiled = lambda w: jnp.broadcast_to(w[:, None, :], (w.shape[0], SUBLANES, w.shape[1]))
    return pl.pallas_call(
        functools.partial(_proj_in_kernel, widths=(ca, diff, conf), tk=tk),
        grid=(b, nt),
        in_specs=[tok(d), row(d), tok(LANES), tok(LANES),
                  pl.BlockSpec((d, in_w), lambda bi, i: (0, 0)), row(2 * conf),
                  taps(SHORT_K, ca), taps(CONF_K, conf), row(conf), row(conf), row(conf)],
        out_specs=[tok(ca), tok(conf),
                   pl.BlockSpec((1, N_HEADS, 2 * HEAD_DIM, tm), lambda bi, i: (bi, 0, 0, i)),
                   tok(diff),
                   pl.BlockSpec((1, N_HEADS, tm // tk, V_ROWS, tk), lambda bi, i: (bi, 0, i, 0, 0))],
        out_shape=[jax.ShapeDtypeStruct((b, s, ca), BF16),
                   jax.ShapeDtypeStruct((b, s, conf), BF16),
                   jax.ShapeDtypeStruct((b, N_HEADS, 2 * HEAD_DIM, s), BF16),
                   jax.ShapeDtypeStruct((b, s, diff), BF16),
                   jax.ShapeDtypeStruct((b, N_HEADS, s // tk, V_ROWS, tk), BF16)],
        scratch_shapes=[pltpu.VMEM((tm, ca), F32), pltpu.VMEM((tm, 3 * diff), F32),
                        pltpu.VMEM((len(_conv_shifts(SHORT_HALO, SHORT_K)), SHORT_HALO + tm, ca), F32),
                        pltpu.VMEM((len(_conv_shifts(CONF_HALO, CONF_K)), CONF_HALO + tm, conf), F32)],
        compiler_params=pltpu.CompilerParams(
            dimension_semantics=("arbitrary", "arbitrary"), vmem_limit_bytes=VMEM_LIMIT),
        name="proj_in",
    )(x, g, cos_t, sin_t, w_in, glu_b, sublane_tiled(scw), sublane_tiled(dww), dwb, lng, lnb)


def _attn_kernel(lq1_ref, lk1_ref, lq2_ref, lk2_ref, g_ref, qT_ref, k_ref, vT_ref,
                 o_ref, q2T_ref, kmax_ref, *state, tq, tk, lam_init):
    c_refs, m_refs, acc_refs = (state[n * N_HEADS:(n + 1) * N_HEADS] for n in range(3))
    i = pl.program_id(1)
    ratio = tk // tq
    n_full = i // ratio
    sub = i % ratio
    col = lax.broadcasted_iota(jnp.int32, (1, 2 * tq), 1)

    @pl.when(i == 0)
    def _():
        lane_r = lax.broadcasted_iota(jnp.int32, (LANES, LANES), 0)
        lane_c = lax.broadcasted_iota(jnp.int32, (LANES, LANES), 1)
        same_map = ((lane_r < HEAD_DIM) == (lane_c < HEAD_DIM)).astype(BF16)

        def chunk(r, best):
            rows = pl.ds(pl.multiple_of(r * tk, tk), tk)
            out = []
            for hh in range(N_HEADS):
                kk = k_ref[0, rows, hh * LANES:(hh + 1) * LANES]
                normsq = _dot(kk * kk, same_map)
                out.append(jnp.maximum(best[hh], jnp.max(normsq, axis=0, keepdims=True)))
            return tuple(out)

        best = lax.fori_loop(0, k_ref.shape[1] // tk, chunk,
                             tuple(jnp.zeros((1, LANES), F32) for _ in range(N_HEADS)))
        for hh in range(N_HEADS):
            kmax_ref[hh] = best[hh]

    feat = lax.broadcasted_iota(jnp.int32, (2 * HEAD_DIM, tq), 0)
    for hh in range(N_HEADS):
        qT = qT_ref[0, hh]
        zero = jnp.zeros_like(qT)
        q2T_ref[hh, :, 0:tq] = jnp.where(feat < HEAD_DIM, qT, zero)
        q2T_ref[hh, :, tq:2 * tq] = jnp.where(feat >= HEAD_DIM, qT, zero)
        q2T = q2T_ref[hh].astype(F32)
        qsq = jnp.sum(q2T * q2T, axis=0, keepdims=True)
        ksq = jnp.where(col < tq, kmax_ref[hh][:, 0:1], kmax_ref[hh][:, HEAD_DIM:HEAD_DIM + 1])
        c_refs[hh][...] = jnp.sqrt(qsq * ksq) * SHIFT_MARGIN
        acc_refs[hh][...] = jnp.zeros(acc_refs[hh].shape, F32)

    def scores(key0, nkeys):
        rows = pl.ds(pl.multiple_of(key0, nkeys), nkeys)
        return [_dot(k_ref[0, rows, hh * LANES:(hh + 1) * LANES], q2T_ref[hh])
                for hh in range(N_HEADS)]

    def diagonal_mask(nkeys):
        key = lax.broadcasted_iota(jnp.int32, (nkeys, 2 * tq), 0)
        qcol = lax.broadcasted_iota(jnp.int32, (nkeys, 2 * tq), 1)
        return key <= jnp.where(qcol >= tq, qcol - tq, qcol)

    def fixed_shift_tiles(tiles):
        sTs = [scores(key0, nkeys) for key0, nkeys, _, _ in tiles]
        for hh in range(N_HEADS):
            total = None
            for (key0, nkeys, vT_of, diagonal), sT in zip(tiles, sTs):
                arg = sT[hh] - c_refs[hh][...]
                if diagonal:
                    arg = jnp.where(diagonal_mask(nkeys), arg, MASK_VALUE)
                part = _dot(vT_of(hh), jnp.exp2(arg).astype(BF16))
                total = part if total is None else total + part
            acc_refs[hh][...] += total

    def running_max_tile(key0, nkeys, vT_of, diagonal):
        sTs = scores(key0, nkeys)
        pTs, alphas = [], []
        for hh in range(N_HEADS):
            sT = jnp.where(diagonal_mask(nkeys), sTs[hh], MASK_VALUE) if diagonal else sTs[hh]
            m_old = m_refs[hh][...]
            m_new = jnp.maximum(m_old, jnp.max(sT, axis=0, keepdims=True))
            m_refs[hh][...] = m_new
            alphas.append(jnp.exp2(m_old - m_new))
            pTs.append(jnp.exp2(sT - m_new).astype(BF16))
        for hh in range(N_HEADS):
            acc_refs[hh][...] = alphas[hh] * acc_refs[hh][...] + _dot(vT_of(hh), pTs[hh])

    def full_tile(j):
        return (j * tk, tk, lambda hh: vT_ref[0, hh, j], False)

    def diagonal_chunks(visit):
        for c in range(ratio):
            chunk = (n_full * tk + c * tq, tq,
                     lambda hh, c=c: vT_ref[0, hh, n_full, :, c * tq:(c + 1) * tq])
            if c < ratio - 1:
                pl.when(sub > c)(functools.partial(visit, chunk + (False,)))
            pl.when(sub == c)(functools.partial(visit, chunk + (True,)))

    def pair(p, carry):
        fixed_shift_tiles([full_tile(2 * p), full_tile(2 * p + 1)])
        return carry

    lax.fori_loop(0, n_full // 2, pair, 0)
    pl.when(n_full % 2 == 1)(lambda: fixed_shift_tiles([full_tile(n_full - 1)]))
    diagonal_chunks(lambda t: fixed_shift_tiles([t]))

    smallest = jnp.min(jnp.concatenate(
        [acc_refs[hh][V_DIM:V_DIM + 1, :] for hh in range(N_HEADS)], axis=1))

    @pl.when(jnp.logical_not(smallest >= MIN_ROW_SUM))
    def _():
        for hh in range(N_HEADS):
            m_refs[hh][...] = jnp.full(m_refs[hh].shape, MASK_VALUE, F32)
            acc_refs[hh][...] = jnp.zeros(acc_refs[hh].shape, F32)

        def body(j, carry):
            running_max_tile(*full_tile(j))
            return carry

        lax.fori_loop(0, n_full, body, 0)
        diagonal_chunks(lambda t: running_max_tile(*t))

    lam = (jnp.exp(jnp.sum(lq1_ref[...] * lk1_ref[...], keepdims=True))
           - jnp.exp(jnp.sum(lq2_ref[...] * lk2_ref[...], keepdims=True)) + lam_init)
    for hh in range(N_HEADS):
        acc = acc_refs[hh][...]
        oT = acc[0:V_DIM, :] / acc[V_DIM:V_DIM + 1, :]
        o = (oT[:, 0:tq] - lam * oT[:, tq:2 * tq]).T
        o_ref[0, :, hh * V_DIM:(hh + 1) * V_DIM] = (
            _rms(o, g_ref[...]) * (1.0 - lam_init)).astype(o_ref.dtype)


def _attn(qT, k, vT, lq1, lk1, lq2, lk2, subln_g, *, tq, tk, lam_init):
    b, nh, _, s = qT.shape
    nk = s // tk
    row = lambda width: pl.BlockSpec((1, width), lambda bi, i: (0, 0))
    return pl.pallas_call(
        functools.partial(_attn_kernel, tq=tq, tk=tk, lam_init=lam_init),
        grid=(b, s // tq),
        in_specs=[row(HEAD_DIM), row(HEAD_DIM), row(HEAD_DIM), row(HEAD_DIM), row(V_DIM),
                  pl.BlockSpec((1, nh, 2 * HEAD_DIM, tq), lambda bi, i: (bi, 0, 0, i)),
                  pl.BlockSpec((1, s, nh * 2 * HEAD_DIM), lambda bi, i: (bi, 0, 0)),
                  pl.BlockSpec((1, nh, nk, V_ROWS, tk), lambda bi, i: (bi, 0, 0, 0, 0))],
        out_specs=pl.BlockSpec((1, tq, nh * V_DIM), lambda bi, i: (bi, i, 0)),
        out_shape=jax.ShapeDtypeStruct((b, s, nh * V_DIM), BF16),
        scratch_shapes=([pltpu.VMEM((nh, 2 * HEAD_DIM, 2 * tq), BF16),
                         pltpu.VMEM((nh, 1, LANES), F32)]
                        + [pltpu.VMEM((1, 2 * tq), F32)] * (2 * nh)
                        + [pltpu.VMEM((V_ROWS, 2 * tq), F32)] * nh),
        compiler_params=pltpu.CompilerParams(
            dimension_semantics=("arbitrary", "arbitrary"), vmem_limit_bytes=VMEM_LIMIT),
        name="attn",
    )(lq1, lk1, lq2, lk2, subln_g, qT, k, vT)


def _ffn_kernel(x_ref, ya_ref, yb_ref, yc_ref, wout_ref, g_ref, wg_ref, wu_ref, wd_ref, fg_ref,
                o_ref, act_ref, *, final):
    ca, diff = ya_ref.shape[1], yb_ref.shape[1]
    x = (x_ref[...] + _dot(ya_ref[...], wout_ref[0:ca, :])
         + _dot(yb_ref[...], wout_ref[ca:ca + diff, :])
         + _dot(yc_ref[...], wout_ref[ca + diff:, :]))
    h = _rms(x, g_ref[...]).astype(BF16)
    hidden = wg_ref.shape[1]
    for c0 in range(0, hidden, FFN_CHUNK):
        gate = _dot(h, wg_ref[:, c0:c0 + FFN_CHUNK])
        up = _dot(h, wu_ref[:, c0:c0 + FFN_CHUNK])
        act_ref[:, c0:c0 + FFN_CHUNK] = (gate * jax.nn.sigmoid(gate) * up).astype(BF16)
    y = x + _dot(act_ref[...], wd_ref[...])
    if final:
        y = _rms(y, fg_ref[...])
    o_ref[...] = y


def _ffn(x2d, ya, yb, yc, w_out, g, w_gate, w_up, w_down, final_g, *, tm, final):
    t, d = x2d.shape
    hidden = w_gate.shape[1]
    tok = lambda width: pl.BlockSpec((tm, width), lambda i: (i, 0))
    const = lambda r, c: pl.BlockSpec((r, c), lambda i: (0, 0), pipeline_mode=pl.Buffered(1))
    return pl.pallas_call(
        functools.partial(_ffn_kernel, final=final),
        grid=(t // tm,),
        in_specs=[tok(d), tok(ya.shape[1]), tok(yb.shape[1]), tok(yc.shape[1]), const(d, d),
                  const(1, d), const(d, hidden), const(d, hidden), const(hidden, d), const(1, d)],
        out_specs=tok(d),
        out_shape=jax.ShapeDtypeStruct((t, d), F32),
        scratch_shapes=[pltpu.VMEM((tm, hidden), BF16)],
        compiler_params=pltpu.CompilerParams(
            dimension_semantics=("arbitrary",), vmem_limit_bytes=VMEM_LIMIT),
        name="ffn",
    )(x2d, ya, yb, yc, w_out, g, w_gate, w_up, w_down, final_g)


def _rope_tables(positions):
    half = HEAD_DIM // 2
    inv_freq = 1.0 / (ROPE_THETA ** (jnp.arange(0, HEAD_DIM, 2, dtype=F32) / HEAD_DIM))
    ang = positions.astype(F32)[..., None] * inv_freq
    cos, sin = jnp.cos(ang), jnp.sin(ang)
    reps = LANES // half
    cos_t = jnp.tile(cos, (1, 1, reps))
    sin_t = jnp.tile(jnp.concatenate([-sin, sin], axis=-1), (1, 1, reps // 2))
    return cos_t, sin_t


def _head_major(w, d, ca, diff):
    def reorder(cols):
        return cols.reshape(d, 2, N_HEADS, HEAD_DIM).transpose(0, 2, 1, 3).reshape(d, diff)
    q0 = 3 * ca
    return jnp.concatenate(
        [w[:, :q0], reorder(w[:, q0:q0 + diff]), reorder(w[:, q0 + diff:q0 + 2 * diff]),
         w[:, q0 + 2 * diff:]], axis=1)


def kernel(x, positions, mix_norm_g, w_in, short_conv_w, glu_b, conf_dw_w, conf_dw_b, conf_ln_g, conf_ln_b, lam_q1, lam_k1, lam_q2, lam_k2, diff_subln_g, w_out, ffn_norm_g, w_gate, w_up, w_down, final_norm_g):
    b, s, d = x.shape
    depth = w_in.shape[0]
    ca, diff = d // 4, d // 2
    assert diff == 2 * N_HEADS * HEAD_DIM and N_HEADS * V_DIM == diff
    tm = min(TM_PROJ, s)
    tq, tk = min(TQ, s), min(TK, s)
    assert s % tm == 0 and s % tk == 0 and tk % tq == 0 and tm % tk == 0

    cos_t, sin_t = _rope_tables(positions)
    row = lambda v: v.reshape(1, -1)
    for l in range(depth):
        lam_init = 0.8 - 0.6 * math.exp(-0.3 * l)
        w_in_l = _head_major(w_in[l], d, ca, diff).astype(BF16)
        ya, yc, qT, k, vT = _proj_in(
            x, row(mix_norm_g[l]), cos_t, sin_t, w_in_l, row(glu_b[l]), short_conv_w[l], conf_dw_w[l],
            row(conf_dw_b[l]), row(conf_ln_g[l]), row(conf_ln_b[l]), tm=tm, tk=tk)
        yb = _attn(qT, k, vT, row(lam_q1[l]), row(lam_k1[l]), row(lam_q2[l]), row(lam_k2[l]),
                   row(diff_subln_g[l]), tq=tq, tk=tk, lam_init=lam_init)
        flat = lambda a: a.reshape(b * s, a.shape[-1])
        x = _ffn(flat(x), flat(ya), flat(yb), flat(yc), w_out[l].astype(BF16), row(ffn_norm_g[l]),
                 w_gate[l].astype(BF16), w_up[l].astype(BF16), w_down[l].astype(BF16),
                 row(final_norm_g), tm=tm, final=(l == depth - 1)).reshape(b, s, d)
    return x
```

```python
import functools
import math

import jax
import jax.numpy as jnp
from jax import lax
from jax.experimental import pallas as pl
from jax.experimental.pallas import tpu as pltpu

F32 = jnp.float32
BF16 = jnp.bfloat16

N_HEADS = 4
HEAD_DIM = 64
V_DIM = 2 * HEAD_DIM
SHORT_K = 3
CONF_K = 31
ROPE_THETA = 10000.0
RMS_EPS = 1e-6
LN_EPS = 1e-5

LANES = 128
SUBLANES = 8
BF16_ROWS = 16
VMEM_LIMIT = 56 * 1024 * 1024

TM_PROJ = 512
TM_FFN = 1024
TQ = 256
TK = 512
CONV_ROWS = 32
COPY_ROWS = 128
FFN_CHUNK = 256
SHORT_HALO = SUBLANES
CONF_HALO = 32
V_ROWS = V_DIM + BF16_ROWS
MASK_VALUE = -1e30
SHIFT_MARGIN = 1.0 + 2.0 ** -7
MIN_ROW_SUM = 2.0 ** -80


def _rms(x, g):
    return x * lax.rsqrt(jnp.mean(x * x, axis=-1, keepdims=True) + RMS_EPS) * g


def _dot(a, b):
    return jnp.dot(a, b, preferred_element_type=F32)


def _conv_shifts(halo, taps):
    shifts = {(halo - (taps - 1) + j) % SUBLANES for j in range(taps)}
    return [0] + sorted(shifts - {0})


def _shift_planes(win_ref, halo, taps, tm):
    shifts = _conv_shifts(halo, taps)
    offsets = [halo - (taps - 1) + j for j in range(taps)]
    for p, sh in enumerate(shifts[1:], start=1):
        n = tm + max(o - sh for o in offsets if o % SUBLANES == sh)
        for r0 in range(0, n, COPY_ROWS):
            rows = min(COPY_ROWS, n - r0)
            ext = win_ref[0, pl.ds(r0, rows + SUBLANES), :]
            win_ref[p, pl.ds(r0, rows), :] = pltpu.roll(ext, rows + SUBLANES - sh, 0)[0:rows]


def _conv_rows(win_ref, w_ref, halo, taps, r0):
    shifts = _conv_shifts(halo, taps)
    groups = CONV_ROWS // SUBLANES
    accs = [None] * groups
    for j in range(taps):
        o = halo - (taps - 1) + j
        sh = o % SUBLANES
        w_j = w_ref[j]
        for gi in range(groups):
            term = w_j * win_ref[shifts.index(sh), pl.ds(r0 + o - sh + gi * SUBLANES, SUBLANES), :]
            accs[gi] = term if accs[gi] is None else accs[gi] + term
    return jnp.concatenate(accs, axis=0)


def _proj_in_kernel(x_ref, g_ref, cos_ref, sin_ref, w_ref, glub_ref,
                    scw_ref, dww_ref, dwb_ref, lng_ref, lnb_ref,
                    ya_ref, yc_ref, qT_ref, k_ref, vT_ref,
                    ab_ref, u_ref, wina_ref, winc_ref, *, widths, tk):
    ca, diff, conf = widths
    tm = x_ref.shape[1]
    h = _rms(x_ref[0], g_ref[...]).astype(BF16)

    @pl.when(pl.program_id(1) == 0)
    def _():
        wina_ref[0, 0:SHORT_HALO, :] = jnp.zeros((SHORT_HALO, ca), F32)
        winc_ref[0, 0:CONF_HALO, :] = jnp.zeros((CONF_HALO, conf), F32)

    c_off = 3 * ca + 3 * diff
    u_c = _dot(h, w_ref[:, c_off:c_off + 2 * conf]) + glub_ref[...]
    winc_ref[0, CONF_HALO:CONF_HALO + tm, :] = u_c[:, 0:conf] * jax.nn.sigmoid(u_c[:, conf:2 * conf])
    _shift_planes(winc_ref, CONF_HALO, CONF_K, tm)

    def conformer_rows(r0):
        c = _conv_rows(winc_ref, dww_ref, CONF_HALO, CONF_K, r0) + dwb_ref[...]
        mu = jnp.mean(c, axis=-1, keepdims=True)
        cc = c - mu
        ln = cc * lax.rsqrt(jnp.mean(cc * cc, axis=-1, keepdims=True) + LN_EPS)
        ln = ln * lng_ref[...] + lnb_ref[...]
        yc_ref[0, pl.ds(r0, CONV_ROWS), :] = (ln * jax.nn.sigmoid(ln)).astype(BF16)

    def keep_gate(u_a):
        ab_ref[...] = u_a[:, 0:ca]
        wina_ref[0, SHORT_HALO:SHORT_HALO + tm, :] = u_a[:, ca:2 * ca] * u_a[:, 2 * ca:3 * ca]

    def keep_qkv(seg, u):
        u_ref[:, seg * diff:(seg + 1) * diff] = u

    keep_gate(_dot(h, w_ref[:, 0:3 * ca]))
    for seg in range(3):
        keep_qkv(seg, _dot(h, w_ref[:, 3 * ca + seg * diff:3 * ca + (seg + 1) * diff]))
    row_starts = list(range(0, tm, CONV_ROWS))
    for r0 in row_starts:
        conformer_rows(r0)

    _shift_planes(wina_ref, SHORT_HALO, SHORT_K, tm)
    for r0 in row_starts:
        conv = _conv_rows(wina_ref, scw_ref, SHORT_HALO, SHORT_K, r0)
        ya_ref[0, pl.ds(r0, CONV_ROWS), :] = (ab_ref[pl.ds(r0, CONV_ROWS), :] * conv).astype(BF16)
    wina_ref[0, 0:SHORT_HALO, :] = wina_ref[0, tm:tm + SHORT_HALO, :]
    winc_ref[0, 0:CONF_HALO, :] = winc_ref[0, tm:tm + CONF_HALO, :]

    half_reps = LANES // HEAD_DIM
    cos = jnp.concatenate([cos_ref[0]] * (2 * half_reps), axis=0).T
    sin = jnp.concatenate([-sin_ref[0], sin_ref[0]] * half_reps, axis=0).T
    lane = lax.broadcasted_iota(jnp.int32, (tm, LANES), 1)
    lower_half = (lane % HEAD_DIM) < (HEAD_DIM // 2)

    def rope(u):
        partner = jnp.where(lower_half, pltpu.roll(u, LANES - HEAD_DIM // 2, 1),
                            pltpu.roll(u, HEAD_DIM // 2, 1))
        return u * cos + partner * sin

    scale = HEAD_DIM ** -0.5 * math.log2(math.e)
    ones = jnp.ones((BF16_ROWS, tk), BF16)
    for hh in range(N_HEADS):
        q_h = rope(u_ref[:, hh * LANES:(hh + 1) * LANES]) * scale
        qT_ref[0, hh] = q_h.T.astype(BF16)
        k_h = rope(u_ref[:, diff + hh * LANES:diff + (hh + 1) * LANES])
        k_ref[0, :, hh * LANES:(hh + 1) * LANES] = k_h.astype(BF16)
        v_hT = u_ref[:, 2 * diff + hh * V_DIM:2 * diff + (hh + 1) * V_DIM].T.astype(BF16)
        for t in range(tm // tk):
            vT_ref[0, hh, t, 0:V_DIM, :] = v_hT[:, t * tk:(t + 1) * tk]
            vT_ref[0, hh, t, V_DIM:V_ROWS, :] = ones


def _proj_in(x, g, cos_t, sin_t, w_in, glu_b, scw, dww, dwb, lng, lnb, *, layer, tm, tk):
    b, s, d = x.shape
    ca, diff = d // 4, d // 2
    conf = d - ca - diff
    in_w = w_in.shape[2]
    nt = s // tm
    tok = lambda width: pl.BlockSpec((1, tm, width), lambda bi, i: (bi, i, 0))
    row = lambda width: pl.BlockSpec((1, width), lambda bi, i: (0, 0))
    taps = lambda k, c: pl.BlockSpec((k, SUBLANES, c), lambda bi, i: (0, 0, 0))
    table = pl.BlockSpec((1, HEAD_DIM // 2, tm), lambda bi, i: (bi, 0, i))
    sublane_tiled = lambda w: jnp.broadcast_to(w[:, None, :], (w.shape[0], SUBLANES, w.shape[1]))
    return pl.pallas_call(
        functools.partial(_proj_in_kernel, widths=(ca, diff, conf), tk=tk),
        grid=(b, nt),
        in_specs=[tok(d), row(d), table, table,
                  pl.BlockSpec((None, d, in_w), lambda bi, i: (layer, 0, 0)), row(2 * conf),
                  taps(SHORT_K, ca), taps(CONF_K, conf), row(conf), row(conf), row(conf)],
        out_specs=[tok(ca), tok(conf),
                   pl.BlockSpec((1, N_HEADS, 2 * HEAD_DIM, tm), lambda bi, i: (bi, 0, 0, i)),
                   tok(diff),
                   pl.BlockSpec((1, N_HEADS, tm // tk, V_ROWS, tk), lambda bi, i: (bi, 0, i, 0, 0))],
        out_shape=[jax.ShapeDtypeStruct((b, s, ca), BF16),
                   jax.ShapeDtypeStruct((b, s, conf), BF16),
                   jax.ShapeDtypeStruct((b, N_HEADS, 2 * HEAD_DIM, s), BF16),
                   jax.ShapeDtypeStruct((b, s, diff), BF16),
                   jax.ShapeDtypeStruct((b, N_HEADS, s // tk, V_ROWS, tk), BF16)],
        scratch_shapes=[pltpu.VMEM((tm, ca), F32), pltpu.VMEM((tm, 3 * diff), F32),
                        pltpu.VMEM((len(_conv_shifts(SHORT_HALO, SHORT_K)), SHORT_HALO + tm, ca), F32),
                        pltpu.VMEM((len(_conv_shifts(CONF_HALO, CONF_K)), CONF_HALO + tm, conf), F32)],
        compiler_params=pltpu.CompilerParams(
            dimension_semantics=("arbitrary", "arbitrary"), vmem_limit_bytes=VMEM_LIMIT),
        name="proj_in",
    )(x, g, cos_t, sin_t, w_in, glu_b, sublane_tiled(scw), sublane_tiled(dww), dwb, lng, lnb)


def _attn_kernel(lq1_ref, lk1_ref, lq2_ref, lk2_ref, g_ref, qT_ref, k_ref, vT_ref,
                 o_ref, q2T_ref, kmax_ref, *state, tq, tk, lam_init):
    c_refs, m_refs, acc_refs = (state[n * N_HEADS:(n + 1) * N_HEADS] for n in range(3))
    i = pl.program_id(1)
    ratio = tk // tq
    n_full = i // ratio
    sub = i % ratio
    col = lax.broadcasted_iota(jnp.int32, (1, 2 * tq), 1)

    @pl.when(i == 0)
    def _():
        lane_r = lax.broadcasted_iota(jnp.int32, (LANES, LANES), 0)
        lane_c = lax.broadcasted_iota(jnp.int32, (LANES, LANES), 1)
        same_map = ((lane_r < HEAD_DIM) == (lane_c < HEAD_DIM)).astype(BF16)

        def chunk(r, best):
            rows = pl.ds(pl.multiple_of(r * tk, tk), tk)
            out = []
            for hh in range(N_HEADS):
                kk = k_ref[0, rows, hh * LANES:(hh + 1) * LANES]
                normsq = _dot(kk * kk, same_map)
                out.append(jnp.maximum(best[hh], jnp.max(normsq, axis=0, keepdims=True)))
            return tuple(out)

        best = lax.fori_loop(0, k_ref.shape[1] // tk, chunk,
                             tuple(jnp.zeros((1, LANES), F32) for _ in range(N_HEADS)))
        for hh in range(N_HEADS):
            kmax_ref[hh] = best[hh]

    feat = lax.broadcasted_iota(jnp.int32, (2 * HEAD_DIM, tq), 0)
    for hh in range(N_HEADS):
        qT = qT_ref[0, hh]
        zero = jnp.zeros_like(qT)
        q2T_ref[hh, :, 0:tq] = jnp.where(feat < HEAD_DIM, qT, zero)
        q2T_ref[hh, :, tq:2 * tq] = jnp.where(feat >= HEAD_DIM, qT, zero)
        q2T = q2T_ref[hh].astype(F32)
        qsq = jnp.sum(q2T * q2T, axis=0, keepdims=True)
        ksq = jnp.where(col < tq, kmax_ref[hh][:, 0:1], kmax_ref[hh][:, HEAD_DIM:HEAD_DIM + 1])
        c_refs[hh][...] = jnp.sqrt(qsq * ksq) * SHIFT_MARGIN
        acc_refs[hh][...] = jnp.zeros(acc_refs[hh].shape, F32)

    def scores(key0, nkeys):
        rows = pl.ds(pl.multiple_of(key0, nkeys), nkeys)
        return [_dot(k_ref[0, rows, hh * LANES:(hh + 1) * LANES], q2T_ref[hh])
                for hh in range(N_HEADS)]

    def diagonal_mask(nkeys):
        key = lax.broadcasted_iota(jnp.int32, (nkeys, 2 * tq), 0)
        qcol = lax.broadcasted_iota(jnp.int32, (nkeys, 2 * tq), 1)
        return key <= jnp.where(qcol >= tq, qcol - tq, qcol) + (nkeys - tq)

    def fixed_shift_tiles(tiles):
        sTs = [scores(key0, nkeys) for key0, nkeys, _, _ in tiles]
        for hh in range(N_HEADS):
            total = None
            for (key0, nkeys, vT_of, diagonal), sT in zip(tiles, sTs):
                arg = sT[hh] - c_refs[hh][...]
                if diagonal:
                    arg = jnp.where(diagonal_mask(nkeys), arg, MASK_VALUE)
                part = _dot(vT_of(hh), jnp.exp2(arg).astype(BF16))
                total = part if total is None else total + part
            acc_refs[hh][...] += total

    def running_max_tile(key0, nkeys, vT_of, diagonal):
        sTs = scores(key0, nkeys)
        pTs, alphas = [], []
        for hh in range(N_HEADS):
            sT = jnp.where(diagonal_mask(nkeys), sTs[hh], MASK_VALUE) if diagonal else sTs[hh]
            m_old = m_refs[hh][...]
            m_new = jnp.maximum(m_old, jnp.max(sT, axis=0, keepdims=True))
            m_refs[hh][...] = m_new
            alphas.append(jnp.exp2(m_old - m_new))
            pTs.append(jnp.exp2(sT - m_new).astype(BF16))
        for hh in range(N_HEADS):
            acc_refs[hh][...] = alphas[hh] * acc_refs[hh][...] + _dot(vT_of(hh), pTs[hh])

    def full_tile(j):
        return (j * tk, tk, lambda hh: vT_ref[0, hh, j], False)

    def diagonal_tile(c):
        nkeys = (c + 1) * tq
        return (n_full * tk, nkeys, lambda hh: vT_ref[0, hh, n_full, :, 0:nkeys], True)

    def finalize():
        lam = (jnp.exp(jnp.sum(lq1_ref[...] * lk1_ref[...], keepdims=True))
               - jnp.exp(jnp.sum(lq2_ref[...] * lk2_ref[...], keepdims=True)) + lam_init)
        for hh in range(N_HEADS):
            acc = acc_refs[hh][...]
            oT = acc[0:V_DIM, :] / acc[V_DIM:V_DIM + 1, :]
            o = (oT[:, 0:tq] - lam * oT[:, tq:2 * tq]).T
            o_ref[0, :, hh * V_DIM:(hh + 1) * V_DIM] = (
                _rms(o, g_ref[...]) * (1.0 - lam_init)).astype(o_ref.dtype)

    def pair(p, carry):
        fixed_shift_tiles([full_tile(2 * p), full_tile(2 * p + 1)])
        return carry

    lax.fori_loop(0, n_full // 2, pair, 0)
    for unpaired in range(2):
        for c in range(ratio):
            tiles = [full_tile(n_full - 1)] * unpaired + [diagonal_tile(c)]
            pl.when(jnp.logical_and(n_full % 2 == unpaired, sub == c))(
                functools.partial(fixed_shift_tiles, tiles))
    finalize()

    smallest = jnp.min(jnp.concatenate(
        [acc_refs[hh][V_DIM:V_DIM + 1, :] for hh in range(N_HEADS)], axis=1))

    @pl.when(jnp.logical_not(smallest >= MIN_ROW_SUM))
    def _():
        for hh in range(N_HEADS):
            m_refs[hh][...] = jnp.full(m_refs[hh].shape, MASK_VALUE, F32)
            acc_refs[hh][...] = jnp.zeros(acc_refs[hh].shape, F32)

        def body(j, carry):
            running_max_tile(*full_tile(j))
            return carry

        lax.fori_loop(0, n_full, body, 0)
        for c in range(ratio):
            pl.when(sub == c)(functools.partial(running_max_tile, *diagonal_tile(c)))
        finalize()


def _attn(qT, k, vT, lq1, lk1, lq2, lk2, subln_g, *, tq, tk, lam_init):
    b, nh, _, s = qT.shape
    nk = s // tk
    row = lambda width: pl.BlockSpec((1, width), lambda bi, i: (0, 0))
    return pl.pallas_call(
        functools.partial(_attn_kernel, tq=tq, tk=tk, lam_init=lam_init),
        grid=(b, s // tq),
        in_specs=[row(HEAD_DIM), row(HEAD_DIM), row(HEAD_DIM), row(HEAD_DIM), row(V_DIM),
                  pl.BlockSpec((1, nh, 2 * HEAD_DIM, tq), lambda bi, i: (bi, 0, 0, i)),
                  pl.BlockSpec((1, s, nh * 2 * HEAD_DIM), lambda bi, i: (bi, 0, 0)),
                  pl.BlockSpec((1, nh, nk, V_ROWS, tk), lambda bi, i: (bi, 0, 0, 0, 0))],
        out_specs=pl.BlockSpec((1, tq, nh * V_DIM), lambda bi, i: (bi, i, 0)),
        out_shape=jax.ShapeDtypeStruct((b, s, nh * V_DIM), BF16),
        scratch_shapes=([pltpu.VMEM((nh, 2 * HEAD_DIM, 2 * tq), BF16),
                         pltpu.VMEM((nh, 1, LANES), F32)]
                        + [pltpu.VMEM((1, 2 * tq), F32)] * (2 * nh)
                        + [pltpu.VMEM((V_ROWS, 2 * tq), F32)] * nh),
        compiler_params=pltpu.CompilerParams(
            dimension_semantics=("arbitrary", "arbitrary"), vmem_limit_bytes=VMEM_LIMIT),
        name="attn",
    )(lq1, lk1, lq2, lk2, subln_g, qT, k, vT)


def _ffn_kernel(x_ref, ya_ref, yb_ref, yc_ref, wout_ref, g_ref, wg_ref, wu_ref, wd_ref, fg_ref,
                o_ref, act_ref, *, final):
    ca, diff = ya_ref.shape[1], yb_ref.shape[1]
    x = (x_ref[...] + _dot(ya_ref[...], wout_ref[0:ca, :])
         + _dot(yb_ref[...], wout_ref[ca:ca + diff, :])
         + _dot(yc_ref[...], wout_ref[ca + diff:, :]))
    h = _rms(x, g_ref[...]).astype(BF16)
    hidden = wg_ref.shape[1]
    for c0 in range(0, hidden, FFN_CHUNK):
        gate = _dot(h, wg_ref[:, c0:c0 + FFN_CHUNK])
        up = _dot(h, wu_ref[:, c0:c0 + FFN_CHUNK])
        act_ref[:, c0:c0 + FFN_CHUNK] = (gate * jax.nn.sigmoid(gate) * up).astype(BF16)
    y = x + _dot(act_ref[...], wd_ref[...])
    if final:
        y = _rms(y, fg_ref[...])
    o_ref[...] = y


def _ffn(x2d, ya, yb, yc, w_out, g, w_gate, w_up, w_down, final_g, *, layer, tm, final):
    t, d = x2d.shape
    hidden = w_gate.shape[2]
    tok = lambda width: pl.BlockSpec((tm, width), lambda i: (i, 0))
    const = lambda r, c: pl.BlockSpec((r, c), lambda i: (0, 0), pipeline_mode=pl.Buffered(1))
    weight = lambda r, c: pl.BlockSpec((None, r, c), lambda i: (layer, 0, 0),
                                       pipeline_mode=pl.Buffered(1))
    return pl.pallas_call(
        functools.partial(_ffn_kernel, final=final),
        grid=(t // tm,),
        in_specs=[tok(d), tok(ya.shape[1]), tok(yb.shape[1]), tok(yc.shape[1]), weight(d, d),
                  const(1, d), weight(d, hidden), weight(d, hidden), weight(hidden, d), const(1, d)],
        out_specs=tok(d),
        out_shape=jax.ShapeDtypeStruct((t, d), F32),
        scratch_shapes=[pltpu.VMEM((tm, hidden), BF16)],
        compiler_params=pltpu.CompilerParams(
            dimension_semantics=("arbitrary",), vmem_limit_bytes=VMEM_LIMIT),
        name="ffn",
    )(x2d, ya, yb, yc, w_out, g, w_gate, w_up, w_down, final_g)


def _rope_tables(positions):
    inv_freq = 1.0 / (ROPE_THETA ** (jnp.arange(0, HEAD_DIM, 2, dtype=F32) / HEAD_DIM))
    ang = positions.astype(F32)[:, None, :] * inv_freq[None, :, None]
    return jnp.cos(ang), jnp.sin(ang)


def _head_major(w, ca, diff):
    order = [m * N_HEADS + h for h in range(N_HEADS) for m in range(2)]
    q0 = 3 * ca

    def reorder(c0):
        return [w[..., c0 + blk * HEAD_DIM:c0 + (blk + 1) * HEAD_DIM] for blk in order]

    return jnp.concatenate(
        [w[..., :q0]] + reorder(q0) + reorder(q0 + diff) + [w[..., q0 + 2 * diff:]], axis=-1)


def kernel(x, positions, mix_norm_g, w_in, short_conv_w, glu_b, conf_dw_w, conf_dw_b, conf_ln_g, conf_ln_b, lam_q1, lam_k1, lam_q2, lam_k2, diff_subln_g, w_out, ffn_norm_g, w_gate, w_up, w_down, final_norm_g):
    b, s, d = x.shape
    depth = w_in.shape[0]
    ca, diff = d // 4, d // 2
    assert diff == 2 * N_HEADS * HEAD_DIM and N_HEADS * V_DIM == diff
    tm = min(TM_PROJ, s)
    tm_ffn = min(TM_FFN, b * s)
    tq, tk = min(TQ, s), min(TK, s)
    assert s % tm == 0 and s % tk == 0 and tk % tq == 0 and tm % tk == 0 and (b * s) % tm_ffn == 0

    cos_t, sin_t = _rope_tables(positions)
    w_in_b = _head_major(w_in, ca, diff).astype(BF16)
    w_out_b, w_gate_b, w_up_b, w_down_b = (w.astype(BF16) for w in (w_out, w_gate, w_up, w_down))
    row = lambda v: v.reshape(1, -1)
    flat = lambda a: a.reshape(b * s, a.shape[-1])
    for l in range(depth):
        lam_init = 0.8 - 0.6 * math.exp(-0.3 * l)
        ya, yc, qT, k, vT = _proj_in(
            x, row(mix_norm_g[l]), cos_t, sin_t, w_in_b, row(glu_b[l]), short_conv_w[l], conf_dw_w[l],
            row(conf_dw_b[l]), row(conf_ln_g[l]), row(conf_ln_b[l]), layer=l, tm=tm, tk=tk)
        yb = _attn(qT, k, vT, row(lam_q1[l]), row(lam_k1[l]), row(lam_q2[l]), row(lam_k2[l]),
                   row(diff_subln_g[l]), tq=tq, tk=tk, lam_init=lam_init)
        x = _ffn(flat(x), flat(ya), flat(yb), flat(yc), w_out_b, row(ffn_norm_g[l]),
                 w_gate_b, w_up_b, w_down_b, row(final_norm_g),
                 layer=l, tm=tm_ffn, final=(l == depth - 1)).reshape(b, s, d)
    return x
```

```python
import functools
import math

import jax
import jax.numpy as jnp
from jax import lax
from jax.experimental import pallas as pl
from jax.experimental.pallas import tpu as pltpu

F32 = jnp.float32
BF16 = jnp.bfloat16

N_HEADS = 4
HEAD_DIM = 64
V_DIM = 2 * HEAD_DIM
SHORT_K = 3
CONF_K = 31
ROPE_THETA = 10000.0
RMS_EPS = 1e-6
LN_EPS = 1e-5

LANES = 128
SUBLANES = 8
VMEM_LIMIT = 56 * 1024 * 1024

TM_PROJ = 512
TM_FFN = 1024
TQ = 256
TK = 512
TILES_PER_TRIP = {512: 1, 256: 2}
CONV_ROWS = 32
COPY_ROWS = 128
FFN_CHUNK = 256
SHORT_HALO = SUBLANES
CONF_HALO = 32
MASK_VALUE = -1e30
SHIFT_MARGIN = 1.0 + 2.0 ** -7
MIN_ROW_SUM = 2.0 ** -80


def _rms(x, g):
    return x * lax.rsqrt(jnp.mean(x * x, axis=-1, keepdims=True) + RMS_EPS) * g


def _dot(a, b):
    return jnp.dot(a, b, preferred_element_type=F32)


def _conv_shifts(halo, taps):
    shifts = {(halo - (taps - 1) + j) % SUBLANES for j in range(taps)}
    return [0] + sorted(shifts - {0})


def _shift_planes(win_ref, halo, taps, r_lo, r_hi):
    shifts = _conv_shifts(halo, taps)
    offsets = [halo - (taps - 1) + j for j in range(taps)]
    for p, sh in enumerate(shifts[1:], start=1):
        aligned = [o - sh for o in offsets if o % SUBLANES == sh]
        lo, hi = r_lo + min(aligned), r_hi + max(aligned)
        for r0 in range(lo, hi, COPY_ROWS):
            rows = min(COPY_ROWS, hi - r0)
            ext = win_ref[0, pl.ds(r0, rows + SUBLANES), :]
            win_ref[p, pl.ds(r0, rows), :] = pltpu.roll(ext, rows + SUBLANES - sh, 0)[0:rows]


def _conv_rows(win_ref, w_ref, halo, taps, r0):
    shifts = _conv_shifts(halo, taps)
    groups = CONV_ROWS // SUBLANES
    accs = [None] * groups
    for j in range(taps):
        o = halo - (taps - 1) + j
        sh = o % SUBLANES
        w_j = w_ref[j]
        for gi in range(groups):
            term = w_j * win_ref[shifts.index(sh), pl.ds(r0 + o - sh + gi * SUBLANES, SUBLANES), :]
            accs[gi] = term if accs[gi] is None else accs[gi] + term
    return jnp.concatenate(accs, axis=0)


def _proj_in_kernel(x_ref, g_ref, cos_ref, sin_ref, w_ref, glub_ref,
                    scw_ref, dww_ref, dwb_ref, lng_ref, lnb_ref,
                    ya_ref, yc_ref, qT_ref, k_ref, vT_ref,
                    ab_ref, u_ref, wina_ref, winc_ref, *, widths, tk):
    ca, diff, conf = widths
    tm = x_ref.shape[1]
    h = _rms(x_ref[0], g_ref[...]).astype(BF16)

    @pl.when(pl.program_id(1) == 0)
    def _():
        wina_ref[0, 0:SHORT_HALO, :] = jnp.zeros((SHORT_HALO, ca), F32)
        winc_ref[0, 0:CONF_HALO, :] = jnp.zeros((CONF_HALO, conf), F32)

    def conformer_rows(r0):
        c = _conv_rows(winc_ref, dww_ref, CONF_HALO, CONF_K, r0) + dwb_ref[...]
        mu = jnp.mean(c, axis=-1, keepdims=True)
        cc = c - mu
        ln = cc * lax.rsqrt(jnp.mean(cc * cc, axis=-1, keepdims=True) + LN_EPS)
        ln = ln * lng_ref[...] + lnb_ref[...]
        yc_ref[0, pl.ds(r0, CONV_ROWS), :] = (ln * jax.nn.sigmoid(ln)).astype(BF16)

    def keep_gate(u_a):
        ab_ref[...] = u_a[:, 0:ca]
        wina_ref[0, SHORT_HALO:SHORT_HALO + tm, :] = u_a[:, ca:2 * ca] * u_a[:, 2 * ca:3 * ca]

    def keep_qkv(seg, u):
        u_ref[:, seg * diff:(seg + 1) * diff] = u

    large = [lambda: keep_gate(_dot(h, w_ref[:, 0:3 * ca]))] + [
        functools.partial(lambda seg: keep_qkv(
            seg, _dot(h, w_ref[:, 3 * ca + seg * diff:3 * ca + (seg + 1) * diff])), seg)
        for seg in range(3)]
    blk = tm // len(large)
    c_off = 3 * ca + 3 * diff
    for gi, big_projection in enumerate(large):
        lo, hi = gi * blk, (gi + 1) * blk
        u_c = _dot(h[lo:hi], w_ref[:, c_off:c_off + 2 * conf]) + glub_ref[...]
        winc_ref[0, CONF_HALO + lo:CONF_HALO + hi, :] = u_c[:, 0:conf] * jax.nn.sigmoid(u_c[:, conf:2 * conf])
        _shift_planes(winc_ref, CONF_HALO, CONF_K, lo, hi)
        for r0 in range(lo, hi, CONV_ROWS):
            conformer_rows(r0)
        big_projection()

    row_starts = list(range(0, tm, CONV_ROWS))
    _shift_planes(wina_ref, SHORT_HALO, SHORT_K, 0, tm)
    for r0 in row_starts:
        conv = _conv_rows(wina_ref, scw_ref, SHORT_HALO, SHORT_K, r0)
        ya_ref[0, pl.ds(r0, CONV_ROWS), :] = (ab_ref[pl.ds(r0, CONV_ROWS), :] * conv).astype(BF16)
    wina_ref[0, 0:SHORT_HALO, :] = wina_ref[0, tm:tm + SHORT_HALO, :]
    winc_ref[0, 0:CONF_HALO, :] = winc_ref[0, tm:tm + CONF_HALO, :]

    half_reps = LANES // HEAD_DIM
    cos = jnp.concatenate([cos_ref[0]] * (2 * half_reps), axis=0).T
    sin = jnp.concatenate([-sin_ref[0], sin_ref[0]] * half_reps, axis=0).T
    lane = lax.broadcasted_iota(jnp.int32, (tm, LANES), 1)
    lower_half = (lane % HEAD_DIM) < (HEAD_DIM // 2)

    def rope(u):
        partner = jnp.where(lower_half, pltpu.roll(u, LANES - HEAD_DIM // 2, 1),
                            pltpu.roll(u, HEAD_DIM // 2, 1))
        return u * cos + partner * sin

    scale = HEAD_DIM ** -0.5 * math.log2(math.e)
    for hh in range(N_HEADS):
        q_h = rope(u_ref[:, hh * LANES:(hh + 1) * LANES]) * scale
        qT_ref[0, hh] = q_h.T.astype(BF16)
        k_h = rope(u_ref[:, diff + hh * LANES:diff + (hh + 1) * LANES])
        k_ref[0, :, hh * LANES:(hh + 1) * LANES] = k_h.astype(BF16)
        v_hT = u_ref[:, 2 * diff + hh * V_DIM:2 * diff + (hh + 1) * V_DIM].T.astype(BF16)
        for t in range(tm // tk):
            vT_ref[0, hh, t] = v_hT[:, t * tk:(t + 1) * tk]


def _proj_in(x, g, cos_t, sin_t, w_in, glu_b, scw, dww, dwb, lng, lnb, *, layer, tm, tk):
    b, s, d = x.shape
    ca, diff = d // 4, d // 2
    conf = d - ca - diff
    in_w = w_in.shape[2]
    nt = s // tm
    tok = lambda width: pl.BlockSpec((1, tm, width), lambda bi, i: (bi, i, 0))
    row = lambda width: pl.BlockSpec((1, width), lambda bi, i: (0, 0))
    taps = lambda k, c: pl.BlockSpec((k, SUBLANES, c), lambda bi, i: (0, 0, 0))
    table = pl.BlockSpec((1, HEAD_DIM // 2, tm), lambda bi, i: (bi, 0, i))
    sublane_tiled = lambda w: jnp.broadcast_to(w[:, None, :], (w.shape[0], SUBLANES, w.shape[1]))
    return pl.pallas_call(
        functools.partial(_proj_in_kernel, widths=(ca, diff, conf), tk=tk),
        grid=(b, nt),
        in_specs=[tok(d), row(d), table, table,
                  pl.BlockSpec((None, d, in_w), lambda bi, i: (layer, 0, 0)), row(2 * conf),
                  taps(SHORT_K, ca), taps(CONF_K, conf), row(conf), row(conf), row(conf)],
        out_specs=[tok(ca), tok(conf),
                   pl.BlockSpec((1, N_HEADS, 2 * HEAD_DIM, tm), lambda bi, i: (bi, 0, 0, i)),
                   tok(diff),
                   pl.BlockSpec((1, N_HEADS, tm // tk, V_DIM, tk), lambda bi, i: (bi, 0, i, 0, 0))],
        out_shape=[jax.ShapeDtypeStruct((b, s, ca), BF16),
                   jax.ShapeDtypeStruct((b, s, conf), BF16),
                   jax.ShapeDtypeStruct((b, N_HEADS, 2 * HEAD_DIM, s), BF16),
                   jax.ShapeDtypeStruct((b, s, diff), BF16),
                   jax.ShapeDtypeStruct((b, N_HEADS, s // tk, V_DIM, tk), BF16)],
        scratch_shapes=[pltpu.VMEM((tm, ca), F32), pltpu.VMEM((tm, 3 * diff), F32),
                        pltpu.VMEM((len(_conv_shifts(SHORT_HALO, SHORT_K)), SHORT_HALO + tm, ca), F32),
                        pltpu.VMEM((len(_conv_shifts(CONF_HALO, CONF_K)), CONF_HALO + tm, conf), F32)],
        compiler_params=pltpu.CompilerParams(
            dimension_semantics=("arbitrary", "arbitrary"), vmem_limit_bytes=VMEM_LIMIT),
        name="proj_in",
    )(x, g, cos_t, sin_t, w_in, glu_b, sublane_tiled(scw), sublane_tiled(dww), dwb, lng, lnb)


def _attn_kernel(lq1_ref, lk1_ref, lq2_ref, lk2_ref, g_ref, qT_ref, k_ref, vT_ref,
                 o_ref, q2T_ref, kmax_ref, *state, tq, tk, lam_init):
    c_refs, m_refs, l_refs, acc_refs = (state[n * N_HEADS:(n + 1) * N_HEADS] for n in range(4))
    i = pl.program_id(1)
    ratio = tk // tq
    n_full = i // ratio
    sub = i % ratio
    col = lax.broadcasted_iota(jnp.int32, (1, 2 * tq), 1)

    @pl.when(i == 0)
    def _():
        lane_r = lax.broadcasted_iota(jnp.int32, (LANES, LANES), 0)
        lane_c = lax.broadcasted_iota(jnp.int32, (LANES, LANES), 1)
        same_map = ((lane_r < HEAD_DIM) == (lane_c < HEAD_DIM)).astype(BF16)

        def chunk(r, best):
            rows = pl.ds(pl.multiple_of(r * tk, tk), tk)
            out = []
            for hh in range(N_HEADS):
                kk = k_ref[0, rows, hh * LANES:(hh + 1) * LANES]
                normsq = _dot(kk * kk, same_map)
                out.append(jnp.maximum(best[hh], jnp.max(normsq, axis=0, keepdims=True)))
            return tuple(out)

        best = lax.fori_loop(0, k_ref.shape[1] // tk, chunk,
                             tuple(jnp.zeros((1, LANES), F32) for _ in range(N_HEADS)))
        for hh in range(N_HEADS):
            kmax_ref[hh] = best[hh]

    feat = lax.broadcasted_iota(jnp.int32, (2 * HEAD_DIM, tq), 0)
    for hh in range(N_HEADS):
        qT = qT_ref[0, hh]
        zero = jnp.zeros_like(qT)
        q2T_ref[hh, :, 0:tq] = jnp.where(feat < HEAD_DIM, qT, zero)
        q2T_ref[hh, :, tq:2 * tq] = jnp.where(feat >= HEAD_DIM, qT, zero)
        q2T = q2T_ref[hh].astype(F32)
        qsq = jnp.sum(q2T * q2T, axis=0, keepdims=True)
        ksq = jnp.where(col < tq, kmax_ref[hh][:, 0:1], kmax_ref[hh][:, HEAD_DIM:HEAD_DIM + 1])
        c_refs[hh][...] = jnp.sqrt(qsq * ksq) * SHIFT_MARGIN
        l_refs[hh][...] = jnp.zeros(l_refs[hh].shape, F32)
        acc_refs[hh][...] = jnp.zeros(acc_refs[hh].shape, F32)

    def scores(key0, nkeys):
        rows = pl.ds(pl.multiple_of(key0, nkeys), nkeys)
        return [_dot(k_ref[0, rows, hh * LANES:(hh + 1) * LANES], q2T_ref[hh])
                for hh in range(N_HEADS)]

    def diagonal_mask(nkeys):
        key = lax.broadcasted_iota(jnp.int32, (nkeys, 2 * tq), 0)
        qcol = lax.broadcasted_iota(jnp.int32, (nkeys, 2 * tq), 1)
        return key <= jnp.where(qcol >= tq, qcol - tq, qcol) + (nkeys - tq)

    def fixed_shift_tiles(tiles):
        sTs = [scores(key0, nkeys) for key0, nkeys, _, _ in tiles]
        for hh in range(N_HEADS):
            total, row_sum = None, None
            for (key0, nkeys, vT_of, diagonal), sT in zip(tiles, sTs):
                arg = sT[hh] - c_refs[hh][...]
                if diagonal:
                    arg = jnp.where(diagonal_mask(nkeys), arg, MASK_VALUE)
                p = jnp.exp2(arg)
                part = _dot(vT_of(hh), p.astype(BF16))
                p_sum = jnp.sum(p, axis=0, keepdims=True)
                total = part if total is None else total + part
                row_sum = p_sum if row_sum is None else row_sum + p_sum
            acc_refs[hh][...] += total
            l_refs[hh][...] += row_sum

    def running_max_tile(key0, nkeys, vT_of, diagonal):
        sTs = scores(key0, nkeys)
        pTs, alphas = [], []
        for hh in range(N_HEADS):
            sT = jnp.where(diagonal_mask(nkeys), sTs[hh], MASK_VALUE) if diagonal else sTs[hh]
            m_old = m_refs[hh][...]
            m_new = jnp.maximum(m_old, jnp.max(sT, axis=0, keepdims=True))
            m_refs[hh][...] = m_new
            alphas.append(jnp.exp2(m_old - m_new))
            p = jnp.exp2(sT - m_new)
            l_refs[hh][...] = alphas[hh] * l_refs[hh][...] + jnp.sum(p, axis=0, keepdims=True)
            pTs.append(p.astype(BF16))
        for hh in range(N_HEADS):
            acc_refs[hh][...] = alphas[hh] * acc_refs[hh][...] + _dot(vT_of(hh), pTs[hh])

    def full_tile(j):
        return (j * tk, tk, lambda hh: vT_ref[0, hh, j], False)

    def diagonal_tile(c):
        nkeys = (c + 1) * tq
        return (n_full * tk, nkeys, lambda hh: vT_ref[0, hh, n_full, :, 0:nkeys], True)

    def finalize():
        lam = (jnp.exp(jnp.sum(lq1_ref[...] * lk1_ref[...], keepdims=True))
               - jnp.exp(jnp.sum(lq2_ref[...] * lk2_ref[...], keepdims=True)) + lam_init)
        for hh in range(N_HEADS):
            oT = acc_refs[hh][...] / l_refs[hh][...]
            o = (oT[:, 0:tq] - lam * oT[:, tq:2 * tq]).T
            o_ref[0, :, hh * V_DIM:(hh + 1) * V_DIM] = (
                _rms(o, g_ref[...]) * (1.0 - lam_init)).astype(o_ref.dtype)

    per_trip = TILES_PER_TRIP[tq]

    def trip(p, carry):
        fixed_shift_tiles([full_tile(per_trip * p + t) for t in range(per_trip)])
        return carry

    lax.fori_loop(0, n_full // per_trip, trip, 0)
    for left in range(per_trip):
        for c in range(ratio):
            tiles = [full_tile(n_full - left + t) for t in range(left)] + [diagonal_tile(c)]
            pl.when(jnp.logical_and(n_full % per_trip == left, sub == c))(
                functools.partial(fixed_shift_tiles, tiles))
    finalize()

    smallest = jnp.min(jnp.concatenate([l_refs[hh][...] for hh in range(N_HEADS)], axis=1))

    @pl.when(jnp.logical_not(smallest >= MIN_ROW_SUM))
    def _():
        for hh in range(N_HEADS):
            m_refs[hh][...] = jnp.full(m_refs[hh].shape, MASK_VALUE, F32)
            l_refs[hh][...] = jnp.zeros(l_refs[hh].shape, F32)
            acc_refs[hh][...] = jnp.zeros(acc_refs[hh].shape, F32)

        def body(j, carry):
            running_max_tile(*full_tile(j))
            return carry

        lax.fori_loop(0, n_full, body, 0)
        for c in range(ratio):
            pl.when(sub == c)(functools.partial(running_max_tile, *diagonal_tile(c)))
        finalize()


def _attn(qT, k, vT, lq1, lk1, lq2, lk2, subln_g, *, tq, tk, lam_init):
    b, nh, _, s = qT.shape
    nk = s // tk
    row = lambda width: pl.BlockSpec((1, width), lambda bi, i: (0, 0))
    return pl.pallas_call(
        functools.partial(_attn_kernel, tq=tq, tk=tk, lam_init=lam_init),
        grid=(b, s // tq),
        in_specs=[row(HEAD_DIM), row(HEAD_DIM), row(HEAD_DIM), row(HEAD_DIM), row(V_DIM),
                  pl.BlockSpec((1, nh, 2 * HEAD_DIM, tq), lambda bi, i: (bi, 0, 0, i)),
                  pl.BlockSpec((1, s, nh * 2 * HEAD_DIM), lambda bi, i: (bi, 0, 0)),
                  pl.BlockSpec((1, nh, nk, V_DIM, tk), lambda bi, i: (bi, 0, 0, 0, 0))],
        out_specs=pl.BlockSpec((1, tq, nh * V_DIM), lambda bi, i: (bi, i, 0)),
        out_shape=jax.ShapeDtypeStruct((b, s, nh * V_DIM), BF16),
        scratch_shapes=([pltpu.VMEM((nh, 2 * HEAD_DIM, 2 * tq), BF16),
                         pltpu.VMEM((nh, 1, LANES), F32)]
                        + [pltpu.VMEM((1, 2 * tq), F32)] * (3 * nh)
                        + [pltpu.VMEM((V_DIM, 2 * tq), F32)] * nh),
        compiler_params=pltpu.CompilerParams(
            dimension_semantics=("arbitrary", "arbitrary"), vmem_limit_bytes=VMEM_LIMIT),
        name="attn",
    )(lq1, lk1, lq2, lk2, subln_g, qT, k, vT)


def _ffn_kernel(x_ref, ya_ref, yb_ref, yc_ref, wout_ref, g_ref, wg_ref, wu_ref, wd_ref, fg_ref,
                o_ref, act_ref, *, final):
    ca, diff = ya_ref.shape[1], yb_ref.shape[1]
    x = (x_ref[...] + _dot(ya_ref[...], wout_ref[0:ca, :])
         + _dot(yb_ref[...], wout_ref[ca:ca + diff, :])
         + _dot(yc_ref[...], wout_ref[ca + diff:, :]))
    h = _rms(x, g_ref[...]).astype(BF16)
    hidden = wg_ref.shape[1]
    for c0 in range(0, hidden, FFN_CHUNK):
        gate = _dot(h, wg_ref[:, c0:c0 + FFN_CHUNK])
        up = _dot(h, wu_ref[:, c0:c0 + FFN_CHUNK])
        act_ref[:, c0:c0 + FFN_CHUNK] = (gate * jax.nn.sigmoid(gate) * up).astype(BF16)
    y = x + _dot(act_ref[...], wd_ref[...])
    if final:
        y = _rms(y, fg_ref[...])
    o_ref[...] = y


def _ffn(x2d, ya, yb, yc, w_out, g, w_gate, w_up, w_down, final_g, *, layer, tm, final):
    t, d = x2d.shape
    hidden = w_gate.shape[2]
    tok = lambda width: pl.BlockSpec((tm, width), lambda i: (i, 0))
    const = lambda r, c: pl.BlockSpec((r, c), lambda i: (0, 0), pipeline_mode=pl.Buffered(1))
    weight = lambda r, c: pl.BlockSpec((None, r, c), lambda i: (layer, 0, 0),
                                       pipeline_mode=pl.Buffered(1))
    return pl.pallas_call(
        functools.partial(_ffn_kernel, final=final),
        grid=(t // tm,),
        in_specs=[tok(d), tok(ya.shape[1]), tok(yb.shape[1]), tok(yc.shape[1]), weight(d, d),
                  const(1, d), weight(d, hidden), weight(d, hidden), weight(hidden, d), const(1, d)],
        out_specs=tok(d),
        out_shape=jax.ShapeDtypeStruct((t, d), F32),
        scratch_shapes=[pltpu.VMEM((tm, hidden), BF16)],
        compiler_params=pltpu.CompilerParams(
            dimension_semantics=("arbitrary",), vmem_limit_bytes=VMEM_LIMIT),
        name="ffn",
    )(x2d, ya, yb, yc, w_out, g, w_gate, w_up, w_down, final_g)


def _rope_tables(positions):
    inv_freq = 1.0 / (ROPE_THETA ** (jnp.arange(0, HEAD_DIM, 2, dtype=F32) / HEAD_DIM))
    ang = positions.astype(F32)[:, None, :] * inv_freq[None, :, None]
    return jnp.cos(ang), jnp.sin(ang)


def _head_major(w, ca, diff):
    order = [m * N_HEADS + h for h in range(N_HEADS) for m in range(2)]
    q0 = 3 * ca

    def reorder(c0):
        return [w[..., c0 + blk * HEAD_DIM:c0 + (blk + 1) * HEAD_DIM] for blk in order]

    return jnp.concatenate(
        [w[..., :q0]] + reorder(q0) + reorder(q0 + diff) + [w[..., q0 + 2 * diff:]], axis=-1)


def kernel(x, positions, mix_norm_g, w_in, short_conv_w, glu_b, conf_dw_w, conf_dw_b, conf_ln_g, conf_ln_b, lam_q1, lam_k1, lam_q2, lam_k2, diff_subln_g, w_out, ffn_norm_g, w_gate, w_up, w_down, final_norm_g):
    b, s, d = x.shape
    depth = w_in.shape[0]
    ca, diff = d // 4, d // 2
    assert diff == 2 * N_HEADS * HEAD_DIM and N_HEADS * V_DIM == diff
    tm = min(TM_PROJ, s)
    tm_ffn = min(TM_FFN, b * s)
    tq, tk = min(TQ, s), min(TK, s)
    assert s % tm == 0 and s % tk == 0 and tk % tq == 0 and tm % tk == 0 and (b * s) % tm_ffn == 0

    cos_t, sin_t = _rope_tables(positions)
    w_in_b = _head_major(w_in, ca, diff).astype(BF16)
    w_out_b, w_gate_b, w_up_b, w_down_b = (w.astype(BF16) for w in (w_out, w_gate, w_up, w_down))
    row = lambda v: v.reshape(1, -1)
    flat = lambda a: a.reshape(b * s, a.shape[-1])
    for l in range(depth):
        lam_init = 0.8 - 0.6 * math.exp(-0.3 * l)
        ya, yc, qT, k, vT = _proj_in(
            x, row(mix_norm_g[l]), cos_t, sin_t, w_in_b, row(glu_b[l]), short_conv_w[l], conf_dw_w[l],
            row(conf_dw_b[l]), row(conf_ln_g[l]), row(conf_ln_b[l]), layer=l, tm=tm, tk=tk)
        yb = _attn(qT, k, vT, row(lam_q1[l]), row(lam_k1[l]), row(lam_q2[l]), row(lam_k2[l]),
                   row(diff_subln_g[l]), tq=tq, tk=tk, lam_init=lam_init)
        x = _ffn(flat(x), flat(ya), flat(yb), flat(yc), w_out_b, row(ffn_norm_g[l]),
                 w_gate_b, w_up_b, w_down_b, row(final_norm_g),
                 layer=l, tm=tm_ffn, final=(l == depth - 1)).reshape(b, s, d)
    return x
```

```python
import functools
import math

import jax
import jax.numpy as jnp
from jax import lax
from jax.experimental import pallas as pl
from jax.experimental.pallas import tpu as pltpu

F32 = jnp.float32
BF16 = jnp.bfloat16

N_HEADS = 4
HEAD_DIM = 64
V_DIM = 2 * HEAD_DIM
SHORT_K = 3
CONF_K = 31
ROPE_THETA = 10000.0
RMS_EPS = 1e-6
LN_EPS = 1e-5

LANES = 128
SUBLANES = 8
VMEM_LIMIT = 56 * 1024 * 1024

TM_PROJ = 512
TM_FFN = 1024
TQ = 256
TK = 512
TILES_PER_TRIP = {512: 1, 256: 2}
CONV_ROWS = 32
COPY_ROWS = 128
FFN_CHUNK = 256
SHORT_HALO = SUBLANES
CONF_HALO = 32
MASK_VALUE = -1e30
SHIFT_MARGIN = 1.0 + 2.0 ** -7
MIN_ROW_SUM = 2.0 ** -80


def _rms(x, g):
    return x * lax.rsqrt(jnp.mean(x * x, axis=-1, keepdims=True) + RMS_EPS) * g


def _dot(a, b):
    return jnp.dot(a, b, preferred_element_type=F32)


def _conv_shifts(halo, taps):
    shifts = {(halo - (taps - 1) + j) % SUBLANES for j in range(taps)}
    return [0] + sorted(shifts - {0})


def _shift_planes(win_ref, halo, taps, r_lo, r_hi):
    shifts = _conv_shifts(halo, taps)
    offsets = [halo - (taps - 1) + j for j in range(taps)]
    for p, sh in enumerate(shifts[1:], start=1):
        aligned = [o - sh for o in offsets if o % SUBLANES == sh]
        lo, hi = r_lo + min(aligned), r_hi + max(aligned)
        for r0 in range(lo, hi, COPY_ROWS):
            rows = min(COPY_ROWS, hi - r0)
            ext = win_ref[0, pl.ds(r0, rows + SUBLANES), :]
            win_ref[p, pl.ds(r0, rows), :] = pltpu.roll(ext, rows + SUBLANES - sh, 0)[0:rows]


def _conv_rows(win_ref, w_ref, halo, taps, r0):
    shifts = _conv_shifts(halo, taps)
    groups = CONV_ROWS // SUBLANES
    accs = [None] * groups
    for j in range(taps):
        o = halo - (taps - 1) + j
        sh = o % SUBLANES
        w_j = w_ref[j]
        for gi in range(groups):
            term = w_j * win_ref[shifts.index(sh), pl.ds(r0 + o - sh + gi * SUBLANES, SUBLANES), :]
            accs[gi] = term if accs[gi] is None else accs[gi] + term
    return jnp.concatenate(accs, axis=0)


def _proj_in_kernel(x_ref, g_ref, cos_ref, sin_ref, w_ref, glub_ref,
                    scw_ref, dww_ref, dwb_ref, lng_ref, lnb_ref,
                    ya_ref, yc_ref, qT_ref, k_ref, vT_ref,
                    ab_ref, u_ref, wina_ref, winc_ref, *, widths, tq, tk):
    ca, diff, conf = widths
    tm = x_ref.shape[1]
    h = _rms(x_ref[0], g_ref[...]).astype(BF16)

    @pl.when(pl.program_id(1) == 0)
    def _():
        wina_ref[0, 0:SHORT_HALO, :] = jnp.zeros((SHORT_HALO, ca), F32)
        winc_ref[0, 0:CONF_HALO, :] = jnp.zeros((CONF_HALO, conf), F32)

    def conformer_rows(r0):
        c = _conv_rows(winc_ref, dww_ref, CONF_HALO, CONF_K, r0) + dwb_ref[...]
        mu = jnp.mean(c, axis=-1, keepdims=True)
        cc = c - mu
        ln = cc * lax.rsqrt(jnp.mean(cc * cc, axis=-1, keepdims=True) + LN_EPS)
        ln = ln * lng_ref[...] + lnb_ref[...]
        yc_ref[0, pl.ds(r0, CONV_ROWS), :] = (ln * jax.nn.sigmoid(ln)).astype(BF16)

    def keep_gate(u_a):
        ab_ref[...] = u_a[:, 0:ca]
        wina_ref[0, SHORT_HALO:SHORT_HALO + tm, :] = u_a[:, ca:2 * ca] * u_a[:, 2 * ca:3 * ca]

    def keep_qkv(seg, u):
        u_ref[:, seg * diff:(seg + 1) * diff] = u

    large = [lambda: keep_gate(_dot(h, w_ref[:, 0:3 * ca]))] + [
        functools.partial(lambda seg: keep_qkv(
            seg, _dot(h, w_ref[:, 3 * ca + seg * diff:3 * ca + (seg + 1) * diff])), seg)
        for seg in range(3)]
    blk = tm // len(large)
    c_off = 3 * ca + 3 * diff
    for gi, big_projection in enumerate(large):
        lo, hi = gi * blk, (gi + 1) * blk
        u_c = _dot(h[lo:hi], w_ref[:, c_off:c_off + 2 * conf]) + glub_ref[...]
        winc_ref[0, CONF_HALO + lo:CONF_HALO + hi, :] = u_c[:, 0:conf] * jax.nn.sigmoid(u_c[:, conf:2 * conf])
        _shift_planes(winc_ref, CONF_HALO, CONF_K, lo, hi)
        for r0 in range(lo, hi, CONV_ROWS):
            conformer_rows(r0)
        big_projection()

    row_starts = list(range(0, tm, CONV_ROWS))
    _shift_planes(wina_ref, SHORT_HALO, SHORT_K, 0, tm)
    for r0 in row_starts:
        conv = _conv_rows(wina_ref, scw_ref, SHORT_HALO, SHORT_K, r0)
        ya_ref[0, pl.ds(r0, CONV_ROWS), :] = (ab_ref[pl.ds(r0, CONV_ROWS), :] * conv).astype(BF16)
    wina_ref[0, 0:SHORT_HALO, :] = wina_ref[0, tm:tm + SHORT_HALO, :]
    winc_ref[0, 0:CONF_HALO, :] = winc_ref[0, tm:tm + CONF_HALO, :]

    half_reps = LANES // HEAD_DIM
    cos = jnp.concatenate([cos_ref[0]] * (2 * half_reps), axis=0).T
    sin = jnp.concatenate([-sin_ref[0], sin_ref[0]] * half_reps, axis=0).T
    lane = lax.broadcasted_iota(jnp.int32, (tm, LANES), 1)
    lower_half = (lane % HEAD_DIM) < (HEAD_DIM // 2)

    def rope(u):
        partner = jnp.where(lower_half, pltpu.roll(u, LANES - HEAD_DIM // 2, 1),
                            pltpu.roll(u, HEAD_DIM // 2, 1))
        return u * cos + partner * sin

    scale = HEAD_DIM ** -0.5 * math.log2(math.e)
    for hh in range(N_HEADS):
        q_hT = (rope(u_ref[:, hh * LANES:(hh + 1) * LANES]) * scale).T.astype(BF16)
        for t in range(tm // tq):
            qT_ref[0, hh, t] = q_hT[:, t * tq:(t + 1) * tq]
        k_h = rope(u_ref[:, diff + hh * LANES:diff + (hh + 1) * LANES])
        k_ref[0, :, hh * LANES:(hh + 1) * LANES] = k_h.astype(BF16)
        v_hT = u_ref[:, 2 * diff + hh * V_DIM:2 * diff + (hh + 1) * V_DIM].T.astype(BF16)
        for t in range(tm // tk):
            vT_ref[0, hh, t] = v_hT[:, t * tk:(t + 1) * tk]


def _proj_in(x, g, cos_t, sin_t, w_in, glu_b, scw, dww, dwb, lng, lnb, *, layer, tm, tq, tk):
    b, s, d = x.shape
    ca, diff = d // 4, d // 2
    conf = d - ca - diff
    in_w = w_in.shape[2]
    nt = s // tm
    tok = lambda width: pl.BlockSpec((1, tm, width), lambda bi, i: (bi, i, 0))
    row = lambda width: pl.BlockSpec((1, width), lambda bi, i: (0, 0))
    taps = lambda k, c: pl.BlockSpec((k, SUBLANES, c), lambda bi, i: (0, 0, 0))
    table = pl.BlockSpec((1, HEAD_DIM // 2, tm), lambda bi, i: (bi, 0, i))
    sublane_tiled = lambda w: jnp.broadcast_to(w[:, None, :], (w.shape[0], SUBLANES, w.shape[1]))
    return pl.pallas_call(
        functools.partial(_proj_in_kernel, widths=(ca, diff, conf), tq=tq, tk=tk),
        grid=(b, nt),
        in_specs=[tok(d), row(d), table, table,
                  pl.BlockSpec((None, d, in_w), lambda bi, i: (layer, 0, 0)), row(2 * conf),
                  taps(SHORT_K, ca), taps(CONF_K, conf), row(conf), row(conf), row(conf)],
        out_specs=[tok(ca), tok(conf),
                   pl.BlockSpec((1, N_HEADS, tm // tq, 2 * HEAD_DIM, tq), lambda bi, i: (bi, 0, i, 0, 0)),
                   tok(diff),
                   pl.BlockSpec((1, N_HEADS, tm // tk, V_DIM, tk), lambda bi, i: (bi, 0, i, 0, 0))],
        out_shape=[jax.ShapeDtypeStruct((b, s, ca), BF16),
                   jax.ShapeDtypeStruct((b, s, conf), BF16),
                   jax.ShapeDtypeStruct((b, N_HEADS, s // tq, 2 * HEAD_DIM, tq), BF16),
                   jax.ShapeDtypeStruct((b, s, diff), BF16),
                   jax.ShapeDtypeStruct((b, N_HEADS, s // tk, V_DIM, tk), BF16)],
        scratch_shapes=[pltpu.VMEM((tm, ca), F32), pltpu.VMEM((tm, 3 * diff), F32),
                        pltpu.VMEM((len(_conv_shifts(SHORT_HALO, SHORT_K)), SHORT_HALO + tm, ca), F32),
                        pltpu.VMEM((len(_conv_shifts(CONF_HALO, CONF_K)), CONF_HALO + tm, conf), F32)],
        compiler_params=pltpu.CompilerParams(
            dimension_semantics=("arbitrary", "arbitrary"), vmem_limit_bytes=VMEM_LIMIT),
        name="proj_in",
    )(x, g, cos_t, sin_t, w_in, glu_b, sublane_tiled(scw), sublane_tiled(dww), dwb, lng, lnb)


def _attn_kernel(lq1_ref, lk1_ref, lq2_ref, lk2_ref, g_ref, qT_ref, k_ref, vT_ref,
                 o_ref, q2T_ref, kmax_ref, *state, tq, tk, lam_init):
    c_refs, m_refs, l_refs, acc_refs = (state[n * N_HEADS:(n + 1) * N_HEADS] for n in range(4))
    i = pl.program_id(1)
    ratio = tk // tq
    n_full = i // ratio
    sub = i % ratio
    col = lax.broadcasted_iota(jnp.int32, (1, 2 * tq), 1)

    @pl.when(i == 0)
    def _():
        lane_r = lax.broadcasted_iota(jnp.int32, (LANES, LANES), 0)
        lane_c = lax.broadcasted_iota(jnp.int32, (LANES, LANES), 1)
        same_map = ((lane_r < HEAD_DIM) == (lane_c < HEAD_DIM)).astype(BF16)

        def chunk(r, best):
            rows = pl.ds(pl.multiple_of(r * tk, tk), tk)
            out = []
            for hh in range(N_HEADS):
                kk = k_ref[0, rows, hh * LANES:(hh + 1) * LANES]
                normsq = _dot(kk * kk, same_map)
                out.append(jnp.maximum(best[hh], jnp.max(normsq, axis=0, keepdims=True)))
            return tuple(out)

        best = lax.fori_loop(0, k_ref.shape[1] // tk, chunk,
                             tuple(jnp.zeros((1, LANES), F32) for _ in range(N_HEADS)))
        for hh in range(N_HEADS):
            kmax_ref[hh] = best[hh]

    feat = lax.broadcasted_iota(jnp.int32, (2 * HEAD_DIM, tq), 0)
    for hh in range(N_HEADS):
        qT = qT_ref[0, hh, 0]
        zero = jnp.zeros_like(qT)
        q2T_ref[hh, :, 0:tq] = jnp.where(feat < HEAD_DIM, qT, zero)
        q2T_ref[hh, :, tq:2 * tq] = jnp.where(feat >= HEAD_DIM, qT, zero)
        q2T = q2T_ref[hh].astype(F32)
        qsq = jnp.sum(q2T * q2T, axis=0, keepdims=True)
        ksq = jnp.where(col < tq, kmax_ref[hh][:, 0:1], kmax_ref[hh][:, HEAD_DIM:HEAD_DIM + 1])
        c_refs[hh][...] = jnp.sqrt(qsq * ksq) * SHIFT_MARGIN
        l_refs[hh][...] = jnp.zeros(l_refs[hh].shape, F32)
        acc_refs[hh][...] = jnp.zeros(acc_refs[hh].shape, F32)

    def scores(key0, nkeys):
        rows = pl.ds(pl.multiple_of(key0, nkeys), nkeys)
        return [_dot(k_ref[0, rows, hh * LANES:(hh + 1) * LANES], q2T_ref[hh])
                for hh in range(N_HEADS)]

    def diagonal_mask(nkeys):
        key = lax.broadcasted_iota(jnp.int32, (nkeys, 2 * tq), 0)
        qcol = lax.broadcasted_iota(jnp.int32, (nkeys, 2 * tq), 1)
        return key <= jnp.where(qcol >= tq, qcol - tq, qcol) + (nkeys - tq)

    def fixed_shift_tiles(tiles):
        sTs = [scores(key0, nkeys) for key0, nkeys, _, _ in tiles]
        for hh in range(N_HEADS):
            total, row_sum = None, None
            for (key0, nkeys, vT_of, diagonal), sT in zip(tiles, sTs):
                arg = sT[hh] - c_refs[hh][...]
                if diagonal:
                    arg = jnp.where(diagonal_mask(nkeys), arg, MASK_VALUE)
                p = jnp.exp2(arg)
                part = _dot(vT_of(hh), p.astype(BF16))
                p_sum = jnp.sum(p, axis=0, keepdims=True)
                total = part if total is None else total + part
                row_sum = p_sum if row_sum is None else row_sum + p_sum
            acc_refs[hh][...] += total
            l_refs[hh][...] += row_sum

    def running_max_tile(key0, nkeys, vT_of, diagonal):
        sTs = scores(key0, nkeys)
        pTs, alphas = [], []
        for hh in range(N_HEADS):
            sT = jnp.where(diagonal_mask(nkeys), sTs[hh], MASK_VALUE) if diagonal else sTs[hh]
            m_old = m_refs[hh][...]
            m_new = jnp.maximum(m_old, jnp.max(sT, axis=0, keepdims=True))
            m_refs[hh][...] = m_new
            alphas.append(jnp.exp2(m_old - m_new))
            p = jnp.exp2(sT - m_new)
            l_refs[hh][...] = alphas[hh] * l_refs[hh][...] + jnp.sum(p, axis=0, keepdims=True)
            pTs.append(p.astype(BF16))
        for hh in range(N_HEADS):
            acc_refs[hh][...] = alphas[hh] * acc_refs[hh][...] + _dot(vT_of(hh), pTs[hh])

    def full_tile(j):
        return (j * tk, tk, lambda hh: vT_ref[0, hh, j], False)

    def diagonal_tile(c):
        nkeys = (c + 1) * tq
        return (n_full * tk, nkeys, lambda hh: vT_ref[0, hh, n_full, :, 0:nkeys], True)

    def finalize():
        lam = (jnp.exp(jnp.sum(lq1_ref[...] * lk1_ref[...], keepdims=True))
               - jnp.exp(jnp.sum(lq2_ref[...] * lk2_ref[...], keepdims=True)) + lam_init)
        for hh in range(N_HEADS):
            oT = acc_refs[hh][...] / l_refs[hh][...]
            o = (oT[:, 0:tq] - lam * oT[:, tq:2 * tq]).T
            o_ref[0, :, hh * V_DIM:(hh + 1) * V_DIM] = (
                _rms(o, g_ref[...]) * (1.0 - lam_init)).astype(o_ref.dtype)

    per_trip = TILES_PER_TRIP[tq]

    def trip(p, carry):
        fixed_shift_tiles([full_tile(per_trip * p + t) for t in range(per_trip)])
        return carry

    lax.fori_loop(0, n_full // per_trip, trip, 0)
    for left in range(per_trip):
        for c in range(ratio):
            tiles = [full_tile(n_full - left + t) for t in range(left)] + [diagonal_tile(c)]
            pl.when(jnp.logical_and(n_full % per_trip == left, sub == c))(
                functools.partial(fixed_shift_tiles, tiles))
    finalize()

    smallest = jnp.min(jnp.concatenate([l_refs[hh][...] for hh in range(N_HEADS)], axis=1))

    @pl.when(jnp.logical_not(smallest >= MIN_ROW_SUM))
    def _():
        for hh in range(N_HEADS):
            m_refs[hh][...] = jnp.full(m_refs[hh].shape, MASK_VALUE, F32)
            l_refs[hh][...] = jnp.zeros(l_refs[hh].shape, F32)
            acc_refs[hh][...] = jnp.zeros(acc_refs[hh].shape, F32)

        def body(j, carry):
            running_max_tile(*full_tile(j))
            return carry

        lax.fori_loop(0, n_full, body, 0)
        for c in range(ratio):
            pl.when(sub == c)(functools.partial(running_max_tile, *diagonal_tile(c)))
        finalize()


def _attn(qT, k, vT, lq1, lk1, lq2, lk2, subln_g, *, tq, tk, lam_init):
    b, s, _ = k.shape
    nh = qT.shape[1]
    nk = s // tk
    row = lambda width: pl.BlockSpec((1, width), lambda bi, i: (0, 0))
    return pl.pallas_call(
        functools.partial(_attn_kernel, tq=tq, tk=tk, lam_init=lam_init),
        grid=(b, s // tq),
        in_specs=[row(HEAD_DIM), row(HEAD_DIM), row(HEAD_DIM), row(HEAD_DIM), row(V_DIM),
                  pl.BlockSpec((1, nh, 1, 2 * HEAD_DIM, tq), lambda bi, i: (bi, 0, i, 0, 0)),
                  pl.BlockSpec((1, s, nh * 2 * HEAD_DIM), lambda bi, i: (bi, 0, 0)),
                  pl.BlockSpec((1, nh, nk, V_DIM, tk), lambda bi, i: (bi, 0, 0, 0, 0))],
        out_specs=pl.BlockSpec((1, tq, nh * V_DIM), lambda bi, i: (bi, i, 0)),
        out_shape=jax.ShapeDtypeStruct((b, s, nh * V_DIM), BF16),
        scratch_shapes=([pltpu.VMEM((nh, 2 * HEAD_DIM, 2 * tq), BF16),
                         pltpu.VMEM((nh, 1, LANES), F32)]
                        + [pltpu.VMEM((1, 2 * tq), F32)] * (3 * nh)
                        + [pltpu.VMEM((V_DIM, 2 * tq), F32)] * nh),
        compiler_params=pltpu.CompilerParams(
            dimension_semantics=("arbitrary", "arbitrary"), vmem_limit_bytes=VMEM_LIMIT),
        name="attn",
    )(lq1, lk1, lq2, lk2, subln_g, qT, k, vT)


def _ffn_kernel(x_ref, ya_ref, yb_ref, yc_ref, wout_ref, g_ref, wg_ref, wu_ref, wd_ref, fg_ref,
                o_ref, act_ref, *, final):
    ca, diff = ya_ref.shape[1], yb_ref.shape[1]
    x = (x_ref[...] + _dot(ya_ref[...], wout_ref[0:ca, :])
         + _dot(yb_ref[...], wout_ref[ca:ca + diff, :])
         + _dot(yc_ref[...], wout_ref[ca + diff:, :]))
    h = _rms(x, g_ref[...]).astype(BF16)
    hidden = wg_ref.shape[1]
    for c0 in range(0, hidden, FFN_CHUNK):
        gate = _dot(h, wg_ref[:, c0:c0 + FFN_CHUNK])
        up = _dot(h, wu_ref[:, c0:c0 + FFN_CHUNK])
        act_ref[:, c0:c0 + FFN_CHUNK] = (gate * jax.nn.sigmoid(gate) * up).astype(BF16)
    y = x + _dot(act_ref[...], wd_ref[...])
    if final:
        y = _rms(y, fg_ref[...])
    o_ref[...] = y


def _ffn(x2d, ya, yb, yc, w_out, g, w_gate, w_up, w_down, final_g, *, layer, tm, final):
    t, d = x2d.shape
    hidden = w_gate.shape[2]
    tok = lambda width: pl.BlockSpec((tm, width), lambda i: (i, 0))
    const = lambda r, c: pl.BlockSpec((r, c), lambda i: (0, 0), pipeline_mode=pl.Buffered(1))
    weight = lambda r, c: pl.BlockSpec((None, r, c), lambda i: (layer, 0, 0),
                                       pipeline_mode=pl.Buffered(1))
    return pl.pallas_call(
        functools.partial(_ffn_kernel, final=final),
        grid=(t // tm,),
        in_specs=[tok(d), tok(ya.shape[1]), tok(yb.shape[1]), tok(yc.shape[1]), weight(d, d),
                  const(1, d), weight(d, hidden), weight(d, hidden), weight(hidden, d), const(1, d)],
        out_specs=tok(d),
        out_shape=jax.ShapeDtypeStruct((t, d), F32),
        scratch_shapes=[pltpu.VMEM((tm, hidden), BF16)],
        compiler_params=pltpu.CompilerParams(
            dimension_semantics=("arbitrary",), vmem_limit_bytes=VMEM_LIMIT),
        name="ffn",
    )(x2d, ya, yb, yc, w_out, g, w_gate, w_up, w_down, final_g)


def _rope_tables(positions):
    inv_freq = 1.0 / (ROPE_THETA ** (jnp.arange(0, HEAD_DIM, 2, dtype=F32) / HEAD_DIM))
    ang = positions.astype(F32)[:, None, :] * inv_freq[None, :, None]
    return jnp.cos(ang), jnp.sin(ang)


def _head_major(w, ca, diff):
    order = [m * N_HEADS + h for h in range(N_HEADS) for m in range(2)]
    q0 = 3 * ca

    def reorder(c0):
        return [w[..., c0 + blk * HEAD_DIM:c0 + (blk + 1) * HEAD_DIM] for blk in order]

    return jnp.concatenate(
        [w[..., :q0]] + reorder(q0) + reorder(q0 + diff) + [w[..., q0 + 2 * diff:]], axis=-1)


def kernel(x, positions, mix_norm_g, w_in, short_conv_w, glu_b, conf_dw_w, conf_dw_b, conf_ln_g, conf_ln_b, lam_q1, lam_k1, lam_q2, lam_k2, diff_subln_g, w_out, ffn_norm_g, w_gate, w_up, w_down, final_norm_g):
    b, s, d = x.shape
    depth = w_in.shape[0]
    ca, diff = d // 4, d // 2
    assert diff == 2 * N_HEADS * HEAD_DIM and N_HEADS * V_DIM == diff
    tm = min(TM_PROJ, s)
    tm_ffn = min(TM_FFN, b * s)
    tq, tk = min(TQ, s), min(TK, s)
    assert s % tm == 0 and s % tk == 0 and tk % tq == 0 and tm % tk == 0 and (b * s) % tm_ffn == 0

    cos_t, sin_t = _rope_tables(positions)
    w_in_b = _head_major(w_in, ca, diff).astype(BF16)
    w_out_b, w_gate_b, w_up_b, w_down_b = (w.astype(BF16) for w in (w_out, w_gate, w_up, w_down))
    row = lambda v: v.reshape(1, -1)
    flat = lambda a: a.reshape(b * s, a.shape[-1])
    for l in range(depth):
        lam_init = 0.8 - 0.6 * math.exp(-0.3 * l)
        ya, yc, qT, k, vT = _proj_in(
            x, row(mix_norm_g[l]), cos_t, sin_t, w_in_b, row(glu_b[l]), short_conv_w[l], conf_dw_w[l],
            row(conf_dw_b[l]), row(conf_ln_g[l]), row(conf_ln_b[l]), layer=l, tm=tm, tq=tq, tk=tk)
        yb = _attn(qT, k, vT, row(lam_q1[l]), row(lam_k1[l]), row(lam_q2[l]), row(lam_k2[l]),
                   row(diff_subln_g[l]), tq=tq, tk=tk, lam_init=lam_init)
        x = _ffn(flat(x), flat(ya), flat(yb), flat(yc), w_out_b, row(ffn_norm_g[l]),
                 w_gate_b, w_up_b, w_down_b, row(final_norm_g),
                 layer=l, tm=tm_ffn, final=(l == depth - 1)).reshape(b, s, d)
    return x
```

```python
import functools
import math

import jax
import jax.numpy as jnp
from jax import lax
from jax.experimental import pallas as pl
from jax.experimental.pallas import tpu as pltpu

F32 = jnp.float32
BF16 = jnp.bfloat16

N_HEADS = 4
HEAD_DIM = 64
V_DIM = 2 * HEAD_DIM
SHORT_K = 3
CONF_K = 31
ROPE_THETA = 10000.0
RMS_EPS = 1e-6
LN_EPS = 1e-5

LANES = 128
SUBLANES = 8
VMEM_LIMIT = 56 * 1024 * 1024

TM_PROJ = 1024
TM_FFN = 1024
TQ = 256
TK = 512
TILES_PER_TRIP = {512: 1, 256: 2}
CONV_ROWS = 32
COPY_ROWS = 128
FFN_CHUNK = 256
SHORT_HALO = SUBLANES
CONF_HALO = 32
MASK_VALUE = -1e30
SHIFT_MARGIN = 1.0 + 2.0 ** -7
MIN_ROW_SUM = 2.0 ** -80


def _rms(x, g):
    return x * lax.rsqrt(jnp.mean(x * x, axis=-1, keepdims=True) + RMS_EPS) * g


def _dot(a, b):
    return jnp.dot(a, b, preferred_element_type=F32)


def _conv_shifts(halo, taps):
    shifts = {(halo - (taps - 1) + j) % SUBLANES for j in range(taps)}
    return [0] + sorted(shifts - {0})


def _shift_planes(win_ref, halo, taps, r_lo, r_hi):
    shifts = _conv_shifts(halo, taps)
    offsets = [halo - (taps - 1) + j for j in range(taps)]
    for p, sh in enumerate(shifts[1:], start=1):
        aligned = [o - sh for o in offsets if o % SUBLANES == sh]
        lo, hi = r_lo + min(aligned), r_hi + max(aligned)
        for r0 in range(lo, hi, COPY_ROWS):
            rows = min(COPY_ROWS, hi - r0)
            ext = win_ref[0, pl.ds(r0, rows + SUBLANES), :]
            win_ref[p, pl.ds(r0, rows), :] = pltpu.roll(ext, rows + SUBLANES - sh, 0)[0:rows]


def _conv_rows(win_ref, w_ref, halo, taps, r0):
    shifts = _conv_shifts(halo, taps)
    groups = CONV_ROWS // SUBLANES
    accs = [None] * groups
    for j in range(taps):
        o = halo - (taps - 1) + j
        sh = o % SUBLANES
        w_j = w_ref[j]
        for gi in range(groups):
            term = w_j * win_ref[shifts.index(sh), pl.ds(r0 + o - sh + gi * SUBLANES, SUBLANES), :]
            accs[gi] = term if accs[gi] is None else accs[gi] + term
    return jnp.concatenate(accs, axis=0)


def _proj_in_kernel(x_ref, g_ref, cos_ref, sin_ref, w_ref, glub_ref,
                    scw_ref, dww_ref, dwb_ref, lng_ref, lnb_ref,
                    ya_ref, yc_ref, qT_ref, k_ref, vT_ref,
                    ab_ref, u_ref, wina_ref, winc_ref, *, widths, tq, tk):
    ca, diff, conf = widths
    tm = x_ref.shape[1]
    h = _rms(x_ref[0], g_ref[...]).astype(BF16)

    @pl.when(pl.program_id(1) == 0)
    def _():
        wina_ref[0, 0:SHORT_HALO, :] = jnp.zeros((SHORT_HALO, ca), F32)
        winc_ref[0, 0:CONF_HALO, :] = jnp.zeros((CONF_HALO, conf), F32)

    def conformer_rows(r0):
        c = _conv_rows(winc_ref, dww_ref, CONF_HALO, CONF_K, r0) + dwb_ref[...]
        mu = jnp.mean(c, axis=-1, keepdims=True)
        cc = c - mu
        ln = cc * lax.rsqrt(jnp.mean(cc * cc, axis=-1, keepdims=True) + LN_EPS)
        ln = ln * lng_ref[...] + lnb_ref[...]
        yc_ref[0, pl.ds(r0, CONV_ROWS), :] = (ln * jax.nn.sigmoid(ln)).astype(BF16)

    def keep_gate(u_a):
        ab_ref[...] = u_a[:, 0:ca]
        wina_ref[0, SHORT_HALO:SHORT_HALO + tm, :] = u_a[:, ca:2 * ca] * u_a[:, 2 * ca:3 * ca]

    def keep_qkv(seg, u):
        u_ref[:, seg * diff:(seg + 1) * diff] = u

    large = [lambda: keep_gate(_dot(h, w_ref[:, 0:3 * ca]))] + [
        functools.partial(lambda seg: keep_qkv(
            seg, _dot(h, w_ref[:, 3 * ca + seg * diff:3 * ca + (seg + 1) * diff])), seg)
        for seg in range(3)]
    blk = tm // len(large)
    c_off = 3 * ca + 3 * diff
    for gi, big_projection in enumerate(large):
        lo, hi = gi * blk, (gi + 1) * blk
        u_c = _dot(h[lo:hi], w_ref[:, c_off:c_off + 2 * conf]) + glub_ref[...]
        winc_ref[0, CONF_HALO + lo:CONF_HALO + hi, :] = u_c[:, 0:conf] * jax.nn.sigmoid(u_c[:, conf:2 * conf])
        _shift_planes(winc_ref, CONF_HALO, CONF_K, lo, hi)
        for r0 in range(lo, hi, CONV_ROWS):
            conformer_rows(r0)
        big_projection()

    row_starts = list(range(0, tm, CONV_ROWS))
    _shift_planes(wina_ref, SHORT_HALO, SHORT_K, 0, tm)
    for r0 in row_starts:
        conv = _conv_rows(wina_ref, scw_ref, SHORT_HALO, SHORT_K, r0)
        ya_ref[0, pl.ds(r0, CONV_ROWS), :] = (ab_ref[pl.ds(r0, CONV_ROWS), :] * conv).astype(BF16)
    wina_ref[0, 0:SHORT_HALO, :] = wina_ref[0, tm:tm + SHORT_HALO, :]
    winc_ref[0, 0:CONF_HALO, :] = winc_ref[0, tm:tm + CONF_HALO, :]

    half_reps = LANES // HEAD_DIM
    cos = jnp.concatenate([cos_ref[0]] * (2 * half_reps), axis=0).T
    sin = jnp.concatenate([-sin_ref[0], sin_ref[0]] * half_reps, axis=0).T
    lane = lax.broadcasted_iota(jnp.int32, (tm, LANES), 1)
    lower_half = (lane % HEAD_DIM) < (HEAD_DIM // 2)

    def rope(u):
        partner = jnp.where(lower_half, pltpu.roll(u, LANES - HEAD_DIM // 2, 1),
                            pltpu.roll(u, HEAD_DIM // 2, 1))
        return u * cos + partner * sin

    scale = HEAD_DIM ** -0.5 * math.log2(math.e)
    for hh in range(N_HEADS):
        q_hT = (rope(u_ref[:, hh * LANES:(hh + 1) * LANES]) * scale).T.astype(BF16)
        for t in range(tm // tq):
            qT_ref[0, hh, t] = q_hT[:, t * tq:(t + 1) * tq]
        k_h = rope(u_ref[:, diff + hh * LANES:diff + (hh + 1) * LANES])
        k_ref[0, :, hh * LANES:(hh + 1) * LANES] = k_h.astype(BF16)
        v_hT = u_ref[:, 2 * diff + hh * V_DIM:2 * diff + (hh + 1) * V_DIM].T.astype(BF16)
        for t in range(tm // tk):
            vT_ref[0, hh, t] = v_hT[:, t * tk:(t + 1) * tk]


def _proj_in(x, g, cos_t, sin_t, w_in, glu_b, scw, dww, dwb, lng, lnb, *, layer, tm, tq, tk):
    b, s, d = x.shape
    ca, diff = d // 4, d // 2
    conf = d - ca - diff
    in_w = w_in.shape[2]
    nt = s // tm
    tok = lambda width: pl.BlockSpec((1, tm, width), lambda bi, i: (bi, i, 0))
    row = lambda width: pl.BlockSpec((1, width), lambda bi, i: (0, 0))
    taps = lambda k, c: pl.BlockSpec((k, SUBLANES, c), lambda bi, i: (0, 0, 0))
    table = pl.BlockSpec((1, HEAD_DIM // 2, tm), lambda bi, i: (bi, 0, i))
    sublane_tiled = lambda w: jnp.broadcast_to(w[:, None, :], (w.shape[0], SUBLANES, w.shape[1]))
    return pl.pallas_call(
        functools.partial(_proj_in_kernel, widths=(ca, diff, conf), tq=tq, tk=tk),
        grid=(b, nt),
        in_specs=[tok(d), row(d), table, table,
                  pl.BlockSpec((None, d, in_w), lambda bi, i: (layer, 0, 0)), row(2 * conf),
                  taps(SHORT_K, ca), taps(CONF_K, conf), row(conf), row(conf), row(conf)],
        out_specs=[tok(ca), tok(conf),
                   pl.BlockSpec((1, N_HEADS, tm // tq, 2 * HEAD_DIM, tq), lambda bi, i: (bi, 0, i, 0, 0)),
                   tok(diff),
                   pl.BlockSpec((1, N_HEADS, tm // tk, V_DIM, tk), lambda bi, i: (bi, 0, i, 0, 0))],
        out_shape=[jax.ShapeDtypeStruct((b, s, ca), BF16),
                   jax.ShapeDtypeStruct((b, s, conf), BF16),
                   jax.ShapeDtypeStruct((b, N_HEADS, s // tq, 2 * HEAD_DIM, tq), BF16),
                   jax.ShapeDtypeStruct((b, s, diff), BF16),
                   jax.ShapeDtypeStruct((b, N_HEADS, s // tk, V_DIM, tk), BF16)],
        scratch_shapes=[pltpu.VMEM((tm, ca), F32), pltpu.VMEM((tm, 3 * diff), F32),
                        pltpu.VMEM((len(_conv_shifts(SHORT_HALO, SHORT_K)), SHORT_HALO + tm, ca), F32),
                        pltpu.VMEM((len(_conv_shifts(CONF_HALO, CONF_K)), CONF_HALO + tm, conf), F32)],
        compiler_params=pltpu.CompilerParams(
            dimension_semantics=("arbitrary", "arbitrary"), vmem_limit_bytes=VMEM_LIMIT),
        name="proj_in",
    )(x, g, cos_t, sin_t, w_in, glu_b, sublane_tiled(scw), sublane_tiled(dww), dwb, lng, lnb)


def _attn_kernel(lq1_ref, lk1_ref, lq2_ref, lk2_ref, g_ref, qT_ref, k_ref, vT_ref,
                 o_ref, q2T_ref, kmax_ref, *state, tq, tk, lam_init):
    c_refs, m_refs, l_refs, acc_refs = (state[n * N_HEADS:(n + 1) * N_HEADS] for n in range(4))
    i = pl.program_id(1)
    ratio = tk // tq
    n_full = i // ratio
    sub = i % ratio
    col = lax.broadcasted_iota(jnp.int32, (1, 2 * tq), 1)

    @pl.when(i == 0)
    def _():
        lane_r = lax.broadcasted_iota(jnp.int32, (LANES, LANES), 0)
        lane_c = lax.broadcasted_iota(jnp.int32, (LANES, LANES), 1)
        same_map = ((lane_r < HEAD_DIM) == (lane_c < HEAD_DIM)).astype(BF16)

        def chunk(r, best):
            rows = pl.ds(pl.multiple_of(r * tk, tk), tk)
            out = []
            for hh in range(N_HEADS):
                kk = k_ref[0, rows, hh * LANES:(hh + 1) * LANES]
                normsq = _dot(kk * kk, same_map)
                out.append(jnp.maximum(best[hh], jnp.max(normsq, axis=0, keepdims=True)))
            return tuple(out)

        best = lax.fori_loop(0, k_ref.shape[1] // tk, chunk,
                             tuple(jnp.zeros((1, LANES), F32) for _ in range(N_HEADS)))
        for hh in range(N_HEADS):
            kmax_ref[hh] = best[hh]

    feat = lax.broadcasted_iota(jnp.int32, (2 * HEAD_DIM, tq), 0)
    for hh in range(N_HEADS):
        qT = qT_ref[0, hh, 0]
        zero = jnp.zeros_like(qT)
        q2T_ref[hh, :, 0:tq] = jnp.where(feat < HEAD_DIM, qT, zero)
        q2T_ref[hh, :, tq:2 * tq] = jnp.where(feat >= HEAD_DIM, qT, zero)
        q2T = q2T_ref[hh].astype(F32)
        qsq = jnp.sum(q2T * q2T, axis=0, keepdims=True)
        ksq = jnp.where(col < tq, kmax_ref[hh][:, 0:1], kmax_ref[hh][:, HEAD_DIM:HEAD_DIM + 1])
        c_refs[hh][...] = jnp.sqrt(qsq * ksq) * SHIFT_MARGIN
        l_refs[hh][...] = jnp.zeros(l_refs[hh].shape, F32)
        acc_refs[hh][...] = jnp.zeros(acc_refs[hh].shape, F32)

    def scores(key0, nkeys):
        rows = pl.ds(pl.multiple_of(key0, nkeys), nkeys)
        return [_dot(k_ref[0, rows, hh * LANES:(hh + 1) * LANES], q2T_ref[hh])
                for hh in range(N_HEADS)]

    def diagonal_mask(nkeys):
        key = lax.broadcasted_iota(jnp.int32, (nkeys, 2 * tq), 0)
        qcol = lax.broadcasted_iota(jnp.int32, (nkeys, 2 * tq), 1)
        return key <= jnp.where(qcol >= tq, qcol - tq, qcol) + (nkeys - tq)

    def fixed_shift_tiles(tiles):
        sTs = [scores(key0, nkeys) for key0, nkeys, _, _ in tiles]
        for hh in range(N_HEADS):
            total, row_sum = None, None
            for (key0, nkeys, vT_of, diagonal), sT in zip(tiles, sTs):
                arg = sT[hh] - c_refs[hh][...]
                if diagonal:
                    arg = jnp.where(diagonal_mask(nkeys), arg, MASK_VALUE)
                p = jnp.exp2(arg)
                part = _dot(vT_of(hh), p.astype(BF16))
                p_sum = jnp.sum(p, axis=0, keepdims=True)
                total = part if total is None else total + part
                row_sum = p_sum if row_sum is None else row_sum + p_sum
            acc_refs[hh][...] += total
            l_refs[hh][...] += row_sum

    def running_max_tile(key0, nkeys, vT_of, diagonal):
        sTs = scores(key0, nkeys)
        pTs, alphas = [], []
        for hh in range(N_HEADS):
            sT = jnp.where(diagonal_mask(nkeys), sTs[hh], MASK_VALUE) if diagonal else sTs[hh]
            m_old = m_refs[hh][...]
            m_new = jnp.maximum(m_old, jnp.max(sT, axis=0, keepdims=True))
            m_refs[hh][...] = m_new
            alphas.append(jnp.exp2(m_old - m_new))
            p = jnp.exp2(sT - m_new)
            l_refs[hh][...] = alphas[hh] * l_refs[hh][...] + jnp.sum(p, axis=0, keepdims=True)
            pTs.append(p.astype(BF16))
        for hh in range(N_HEADS):
            acc_refs[hh][...] = alphas[hh] * acc_refs[hh][...] + _dot(vT_of(hh), pTs[hh])

    def full_tile(j):
        return (j * tk, tk, lambda hh: vT_ref[0, hh, j], False)

    def diagonal_tile(c):
        nkeys = (c + 1) * tq
        return (n_full * tk, nkeys, lambda hh: vT_ref[0, hh, n_full, :, 0:nkeys], True)

    def finalize():
        lam = (jnp.exp(jnp.sum(lq1_ref[...] * lk1_ref[...], keepdims=True))
               - jnp.exp(jnp.sum(lq2_ref[...] * lk2_ref[...], keepdims=True)) + lam_init)
        for hh in range(N_HEADS):
            oT = acc_refs[hh][...] / l_refs[hh][...]
            o = (oT[:, 0:tq] - lam * oT[:, tq:2 * tq]).T
            o_ref[0, :, hh * V_DIM:(hh + 1) * V_DIM] = (
                _rms(o, g_ref[...]) * (1.0 - lam_init)).astype(o_ref.dtype)

    per_trip = TILES_PER_TRIP[tq]

    def trip(p, carry):
        fixed_shift_tiles([full_tile(per_trip * p + t) for t in range(per_trip)])
        return carry

    lax.fori_loop(0, n_full // per_trip, trip, 0)
    for left in range(per_trip):
        for c in range(ratio):
            tiles = [full_tile(n_full - left + t) for t in range(left)] + [diagonal_tile(c)]
            pl.when(jnp.logical_and(n_full % per_trip == left, sub == c))(
                functools.partial(fixed_shift_tiles, tiles))
    finalize()

    smallest = jnp.min(jnp.concatenate([l_refs[hh][...] for hh in range(N_HEADS)], axis=1))

    @pl.when(jnp.logical_not(smallest >= MIN_ROW_SUM))
    def _():
        for hh in range(N_HEADS):
            m_refs[hh][...] = jnp.full(m_refs[hh].shape, MASK_VALUE, F32)
            l_refs[hh][...] = jnp.zeros(l_refs[hh].shape, F32)
            acc_refs[hh][...] = jnp.zeros(acc_refs[hh].shape, F32)

        def body(j, carry):
            running_max_tile(*full_tile(j))
            return carry

        lax.fori_loop(0, n_full, body, 0)
        for c in range(ratio):
            pl.when(sub == c)(functools.partial(running_max_tile, *diagonal_tile(c)))
        finalize()


def _attn(qT, k, vT, lq1, lk1, lq2, lk2, subln_g, *, tq, tk, lam_init):
    b, s, _ = k.shape
    nh = qT.shape[1]
    nk = s // tk
    row = lambda width: pl.BlockSpec((1, width), lambda bi, i: (0, 0))
    return pl.pallas_call(
        functools.partial(_attn_kernel, tq=tq, tk=tk, lam_init=lam_init),
        grid=(b, s // tq),
        in_specs=[row(HEAD_DIM), row(HEAD_DIM), row(HEAD_DIM), row(HEAD_DIM), row(V_DIM),
                  pl.BlockSpec((1, nh, 1, 2 * HEAD_DIM, tq), lambda bi, i: (bi, 0, i, 0, 0)),
                  pl.BlockSpec((1, s, nh * 2 * HEAD_DIM), lambda bi, i: (bi, 0, 0)),
                  pl.BlockSpec((1, nh, nk, V_DIM, tk), lambda bi, i: (bi, 0, 0, 0, 0))],
        out_specs=pl.BlockSpec((1, tq, nh * V_DIM), lambda bi, i: (bi, i, 0)),
        out_shape=jax.ShapeDtypeStruct((b, s, nh * V_DIM), BF16),
        scratch_shapes=([pltpu.VMEM((nh, 2 * HEAD_DIM, 2 * tq), BF16),
                         pltpu.VMEM((nh, 1, LANES), F32)]
                        + [pltpu.VMEM((1, 2 * tq), F32)] * (3 * nh)
                        + [pltpu.VMEM((V_DIM, 2 * tq), F32)] * nh),
        compiler_params=pltpu.CompilerParams(
            dimension_semantics=("arbitrary", "arbitrary"), vmem_limit_bytes=VMEM_LIMIT),
        name="attn",
    )(lq1, lk1, lq2, lk2, subln_g, qT, k, vT)


def _ffn_kernel(x_ref, ya_ref, yb_ref, yc_ref, wout_ref, g_ref, wg_ref, wu_ref, wd_ref, fg_ref,
                o_ref, act_ref, *, final):
    ca, diff = ya_ref.shape[1], yb_ref.shape[1]
    x = (x_ref[...] + _dot(ya_ref[...], wout_ref[0:ca, :])
         + _dot(yb_ref[...], wout_ref[ca:ca + diff, :])
         + _dot(yc_ref[...], wout_ref[ca + diff:, :]))
    h = _rms(x, g_ref[...]).astype(BF16)
    hidden = wg_ref.shape[1]
    for c0 in range(0, hidden, FFN_CHUNK):
        gate = _dot(h, wg_ref[:, c0:c0 + FFN_CHUNK])
        up = _dot(h, wu_ref[:, c0:c0 + FFN_CHUNK])
        act_ref[:, c0:c0 + FFN_CHUNK] = (gate * jax.nn.sigmoid(gate) * up).astype(BF16)
    y = x + _dot(act_ref[...], wd_ref[...])
    if final:
        y = _rms(y, fg_ref[...])
    o_ref[...] = y


def _ffn(x2d, ya, yb, yc, w_out, g, w_gate, w_up, w_down, final_g, *, layer, tm, final):
    t, d = x2d.shape
    hidden = w_gate.shape[2]
    tok = lambda width: pl.BlockSpec((tm, width), lambda i: (i, 0))
    const = lambda r, c: pl.BlockSpec((r, c), lambda i: (0, 0), pipeline_mode=pl.Buffered(1))
    weight = lambda r, c: pl.BlockSpec((None, r, c), lambda i: (layer, 0, 0),
                                       pipeline_mode=pl.Buffered(1))
    return pl.pallas_call(
        functools.partial(_ffn_kernel, final=final),
        grid=(t // tm,),
        in_specs=[tok(d), tok(ya.shape[1]), tok(yb.shape[1]), tok(yc.shape[1]), weight(d, d),
                  const(1, d), weight(d, hidden), weight(d, hidden), weight(hidden, d), const(1, d)],
        out_specs=tok(d),
        out_shape=jax.ShapeDtypeStruct((t, d), F32),
        scratch_shapes=[pltpu.VMEM((tm, hidden), BF16)],
        compiler_params=pltpu.CompilerParams(
            dimension_semantics=("arbitrary",), vmem_limit_bytes=VMEM_LIMIT),
        name="ffn",
    )(x2d, ya, yb, yc, w_out, g, w_gate, w_up, w_down, final_g)


def _rope_tables(positions):
    inv_freq = 1.0 / (ROPE_THETA ** (jnp.arange(0, HEAD_DIM, 2, dtype=F32) / HEAD_DIM))
    ang = positions.astype(F32)[:, None, :] * inv_freq[None, :, None]
    return jnp.cos(ang), jnp.sin(ang)


def _head_major(w, ca, diff):
    order = [m * N_HEADS + h for h in range(N_HEADS) for m in range(2)]
    q0 = 3 * ca

    def reorder(c0):
        return [w[..., c0 + blk * HEAD_DIM:c0 + (blk + 1) * HEAD_DIM] for blk in order]

    return jnp.concatenate(
        [w[..., :q0]] + reorder(q0) + reorder(q0 + diff) + [w[..., q0 + 2 * diff:]], axis=-1)


def kernel(x, positions, mix_norm_g, w_in, short_conv_w, glu_b, conf_dw_w, conf_dw_b, conf_ln_g, conf_ln_b, lam_q1, lam_k1, lam_q2, lam_k2, diff_subln_g, w_out, ffn_norm_g, w_gate, w_up, w_down, final_norm_g):
    b, s, d = x.shape
    depth = w_in.shape[0]
    ca, diff = d // 4, d // 2
    assert diff == 2 * N_HEADS * HEAD_DIM and N_HEADS * V_DIM == diff
    tm = min(TM_PROJ, s)
    tm_ffn = min(TM_FFN, b * s)
    tq, tk = min(TQ, s), min(TK, s)
    assert s % tm == 0 and s % tk == 0 and tk % tq == 0 and tm % tk == 0 and (b * s) % tm_ffn == 0

    cos_t, sin_t = _rope_tables(positions)
    w_in_b = _head_major(w_in, ca, diff).astype(BF16)
    w_out_b, w_gate_b, w_up_b, w_down_b = (w.astype(BF16) for w in (w_out, w_gate, w_up, w_down))
    row = lambda v: v.reshape(1, -1)
    flat = lambda a: a.reshape(b * s, a.shape[-1])
    for l in range(depth):
        lam_init = 0.8 - 0.6 * math.exp(-0.3 * l)
        ya, yc, qT, k, vT = _proj_in(
            x, row(mix_norm_g[l]), cos_t, sin_t, w_in_b, row(glu_b[l]), short_conv_w[l], conf_dw_w[l],
            row(conf_dw_b[l]), row(conf_ln_g[l]), row(conf_ln_b[l]), layer=l, tm=tm, tq=tq, tk=tk)
        yb = _attn(qT, k, vT, row(lam_q1[l]), row(lam_k1[l]), row(lam_q2[l]), row(lam_k2[l]),
                   row(diff_subln_g[l]), tq=tq, tk=tk, lam_init=lam_init)
        x = _ffn(flat(x), flat(ya), flat(yb), flat(yc), w_out_b, row(ffn_norm_g[l]),
                 w_gate_b, w_up_b, w_down_b, row(final_norm_g),
                 layer=l, tm=tm_ffn, final=(l == depth - 1)).reshape(b, s, d)
    return x
```

```python
import functools
import math

import jax
import jax.numpy as jnp
from jax import lax
from jax.experimental import pallas as pl
from jax.experimental.pallas import tpu as pltpu

F32 = jnp.float32
BF16 = jnp.bfloat16

N_HEADS = 4
HEAD_DIM = 64
V_DIM = 2 * HEAD_DIM
SHORT_K = 3
CONF_K = 31
ROPE_THETA = 10000.0
RMS_EPS = 1e-6
LN_EPS = 1e-5

LANES = 128
SUBLANES = 8
VMEM_LIMIT = 56 * 1024 * 1024

TM_PROJ = 1024
TM_FFN = 1024
TQ = 256
TK = 512
TILES_PER_TRIP = {512: 1, 256: 3}
CONV_ROWS = 32
COPY_ROWS = 128
FFN_CHUNK = 256
SHORT_HALO = SUBLANES
CONF_HALO = 32
MASK_VALUE = -1e30
SHIFT_MARGIN = 1.0 + 2.0 ** -7
MIN_ROW_SUM = 2.0 ** -80


def _rms(x, g):
    return x * lax.rsqrt(jnp.mean(x * x, axis=-1, keepdims=True) + RMS_EPS) * g


def _dot(a, b):
    return jnp.dot(a, b, preferred_element_type=F32)


def _conv_shifts(halo, taps):
    shifts = {(halo - (taps - 1) + j) % SUBLANES for j in range(taps)}
    return [0] + sorted(shifts - {0})


def _shift_planes(win_ref, halo, taps, r_lo, r_hi):
    shifts = _conv_shifts(halo, taps)
    offsets = [halo - (taps - 1) + j for j in range(taps)]
    for p, sh in enumerate(shifts[1:], start=1):
        aligned = [o - sh for o in offsets if o % SUBLANES == sh]
        lo, hi = r_lo + min(aligned), r_hi + max(aligned)
        for r0 in range(lo, hi, COPY_ROWS):
            rows = min(COPY_ROWS, hi - r0)
            ext = win_ref[0, pl.ds(r0, rows + SUBLANES), :]
            win_ref[p, pl.ds(r0, rows), :] = pltpu.roll(ext, rows + SUBLANES - sh, 0)[0:rows]


def _conv_rows(win_ref, w_ref, halo, taps, r0):
    shifts = _conv_shifts(halo, taps)
    groups = CONV_ROWS // SUBLANES
    accs = [None] * groups
    for j in range(taps):
        o = halo - (taps - 1) + j
        sh = o % SUBLANES
        w_j = w_ref[j]
        for gi in range(groups):
            term = w_j * win_ref[shifts.index(sh), pl.ds(r0 + o - sh + gi * SUBLANES, SUBLANES), :]
            accs[gi] = term if accs[gi] is None else accs[gi] + term
    return jnp.concatenate(accs, axis=0)


def _proj_in_kernel(x_ref, g_ref, cos_ref, sin_ref, w_ref, glub_ref,
                    scw_ref, dww_ref, dwb_ref, lng_ref, lnb_ref,
                    ya_ref, yc_ref, qT_ref, k_ref, vT_ref,
                    ab_ref, u_ref, wina_ref, winc_ref, *, widths, tq, tk):
    ca, diff, conf = widths
    tm = x_ref.shape[1]
    h = _rms(x_ref[0], g_ref[...]).astype(BF16)

    @pl.when(pl.program_id(1) == 0)
    def _():
        wina_ref[0, 0:SHORT_HALO, :] = jnp.zeros((SHORT_HALO, ca), F32)
        winc_ref[0, 0:CONF_HALO, :] = jnp.zeros((CONF_HALO, conf), F32)

    def conformer_rows(r0):
        c = _conv_rows(winc_ref, dww_ref, CONF_HALO, CONF_K, r0) + dwb_ref[...]
        mu = jnp.mean(c, axis=-1, keepdims=True)
        cc = c - mu
        ln = cc * lax.rsqrt(jnp.mean(cc * cc, axis=-1, keepdims=True) + LN_EPS)
        ln = ln * lng_ref[...] + lnb_ref[...]
        yc_ref[0, pl.ds(r0, CONV_ROWS), :] = (ln * jax.nn.sigmoid(ln)).astype(BF16)

    def keep_gate(u_a):
        ab_ref[...] = u_a[:, 0:ca]
        wina_ref[0, SHORT_HALO:SHORT_HALO + tm, :] = u_a[:, ca:2 * ca] * u_a[:, 2 * ca:3 * ca]

    def keep_qkv(seg, u):
        u_ref[:, seg * diff:(seg + 1) * diff] = u

    large = [lambda: keep_gate(_dot(h, w_ref[:, 0:3 * ca]))] + [
        functools.partial(lambda seg: keep_qkv(
            seg, _dot(h, w_ref[:, 3 * ca + seg * diff:3 * ca + (seg + 1) * diff])), seg)
        for seg in range(3)]
    blk = tm // len(large)
    c_off = 3 * ca + 3 * diff
    for gi, big_projection in enumerate(large):
        lo, hi = gi * blk, (gi + 1) * blk
        u_c = _dot(h[lo:hi], w_ref[:, c_off:c_off + 2 * conf]) + glub_ref[...]
        winc_ref[0, CONF_HALO + lo:CONF_HALO + hi, :] = u_c[:, 0:conf] * jax.nn.sigmoid(u_c[:, conf:2 * conf])
        _shift_planes(winc_ref, CONF_HALO, CONF_K, lo, hi)
        for r0 in range(lo, hi, CONV_ROWS):
            conformer_rows(r0)
        big_projection()

    row_starts = list(range(0, tm, CONV_ROWS))
    _shift_planes(wina_ref, SHORT_HALO, SHORT_K, 0, tm)
    for r0 in row_starts:
        conv = _conv_rows(wina_ref, scw_ref, SHORT_HALO, SHORT_K, r0)
        ya_ref[0, pl.ds(r0, CONV_ROWS), :] = (ab_ref[pl.ds(r0, CONV_ROWS), :] * conv).astype(BF16)
    wina_ref[0, 0:SHORT_HALO, :] = wina_ref[0, tm:tm + SHORT_HALO, :]
    winc_ref[0, 0:CONF_HALO, :] = winc_ref[0, tm:tm + CONF_HALO, :]

    half_reps = LANES // HEAD_DIM
    cos = jnp.concatenate([cos_ref[0]] * (2 * half_reps), axis=0).T
    sin = jnp.concatenate([-sin_ref[0], sin_ref[0]] * half_reps, axis=0).T
    lane = lax.broadcasted_iota(jnp.int32, (tm, LANES), 1)
    lower_half = (lane % HEAD_DIM) < (HEAD_DIM // 2)

    def rope(u):
        partner = jnp.where(lower_half, pltpu.roll(u, LANES - HEAD_DIM // 2, 1),
                            pltpu.roll(u, HEAD_DIM // 2, 1))
        return u * cos + partner * sin

    scale = HEAD_DIM ** -0.5 * math.log2(math.e)
    for hh in range(N_HEADS):
        q_hT = (rope(u_ref[:, hh * LANES:(hh + 1) * LANES]) * scale).T.astype(BF16)
        for t in range(tm // tq):
            qT_ref[0, hh, t] = q_hT[:, t * tq:(t + 1) * tq]
        k_h = rope(u_ref[:, diff + hh * LANES:diff + (hh + 1) * LANES])
        k_ref[0, :, hh * LANES:(hh + 1) * LANES] = k_h.astype(BF16)
        v_hT = u_ref[:, 2 * diff + hh * V_DIM:2 * diff + (hh + 1) * V_DIM].T.astype(BF16)
        for t in range(tm // tk):
            vT_ref[0, hh, t] = v_hT[:, t * tk:(t + 1) * tk]


def _proj_in(x, g, cos_t, sin_t, w_in, glu_b, scw, dww, dwb, lng, lnb, *, layer, tm, tq, tk):
    b, s, d = x.shape
    ca, diff = d // 4, d // 2
    conf = d - ca - diff
    in_w = w_in.shape[2]
    nt = s // tm
    tok = lambda width: pl.BlockSpec((1, tm, width), lambda bi, i: (bi, i, 0))
    row = lambda width: pl.BlockSpec((1, width), lambda bi, i: (0, 0))
    taps = lambda k, c: pl.BlockSpec((k, SUBLANES, c), lambda bi, i: (0, 0, 0))
    table = pl.BlockSpec((1, HEAD_DIM // 2, tm), lambda bi, i: (bi, 0, i))
    sublane_tiled = lambda w: jnp.broadcast_to(w[:, None, :], (w.shape[0], SUBLANES, w.shape[1]))
    return pl.pallas_call(
        functools.partial(_proj_in_kernel, widths=(ca, diff, conf), tq=tq, tk=tk),
        grid=(b, nt),
        in_specs=[tok(d), row(d), table, table,
                  pl.BlockSpec((None, d, in_w), lambda bi, i: (layer, 0, 0)), row(2 * conf),
                  taps(SHORT_K, ca), taps(CONF_K, conf), row(conf), row(conf), row(conf)],
        out_specs=[tok(ca), tok(conf),
                   pl.BlockSpec((1, N_HEADS, tm // tq, 2 * HEAD_DIM, tq), lambda bi, i: (bi, 0, i, 0, 0)),
                   tok(diff),
                   pl.BlockSpec((1, N_HEADS, tm // tk, V_DIM, tk), lambda bi, i: (bi, 0, i, 0, 0))],
        out_shape=[jax.ShapeDtypeStruct((b, s, ca), BF16),
                   jax.ShapeDtypeStruct((b, s, conf), BF16),
                   jax.ShapeDtypeStruct((b, N_HEADS, s // tq, 2 * HEAD_DIM, tq), BF16),
                   jax.ShapeDtypeStruct((b, s, diff), BF16),
                   jax.ShapeDtypeStruct((b, N_HEADS, s // tk, V_DIM, tk), BF16)],
        scratch_shapes=[pltpu.VMEM((tm, ca), F32), pltpu.VMEM((tm, 3 * diff), F32),
                        pltpu.VMEM((len(_conv_shifts(SHORT_HALO, SHORT_K)), SHORT_HALO + tm, ca), F32),
                        pltpu.VMEM((len(_conv_shifts(CONF_HALO, CONF_K)), CONF_HALO + tm, conf), F32)],
        compiler_params=pltpu.CompilerParams(
            dimension_semantics=("arbitrary", "arbitrary"), vmem_limit_bytes=VMEM_LIMIT),
        name="proj_in",
    )(x, g, cos_t, sin_t, w_in, glu_b, sublane_tiled(scw), sublane_tiled(dww), dwb, lng, lnb)


def _attn_kernel(lq1_ref, lk1_ref, lq2_ref, lk2_ref, g_ref, qT_ref, k_ref, vT_ref,
                 o_ref, q2T_ref, kmax_ref, *state, tq, tk, lam_init):
    c_refs, m_refs, l_refs, acc_refs = (state[n * N_HEADS:(n + 1) * N_HEADS] for n in range(4))
    i = pl.program_id(1)
    ratio = tk // tq
    n_full = i // ratio
    sub = i % ratio
    col = lax.broadcasted_iota(jnp.int32, (1, 2 * tq), 1)

    @pl.when(i == 0)
    def _():
        lane_r = lax.broadcasted_iota(jnp.int32, (LANES, LANES), 0)
        lane_c = lax.broadcasted_iota(jnp.int32, (LANES, LANES), 1)
        same_map = ((lane_r < HEAD_DIM) == (lane_c < HEAD_DIM)).astype(BF16)

        def chunk(r, best):
            rows = pl.ds(pl.multiple_of(r * tk, tk), tk)
            out = []
            for hh in range(N_HEADS):
                kk = k_ref[0, rows, hh * LANES:(hh + 1) * LANES]
                normsq = _dot(kk * kk, same_map)
                out.append(jnp.maximum(best[hh], jnp.max(normsq, axis=0, keepdims=True)))
            return tuple(out)

        best = lax.fori_loop(0, k_ref.shape[1] // tk, chunk,
                             tuple(jnp.zeros((1, LANES), F32) for _ in range(N_HEADS)))
        for hh in range(N_HEADS):
            kmax_ref[hh] = best[hh]

    feat = lax.broadcasted_iota(jnp.int32, (2 * HEAD_DIM, tq), 0)
    for hh in range(N_HEADS):
        qT = qT_ref[0, hh, 0]
        zero = jnp.zeros_like(qT)
        q2T_ref[hh, :, 0:tq] = jnp.where(feat < HEAD_DIM, qT, zero)
        q2T_ref[hh, :, tq:2 * tq] = jnp.where(feat >= HEAD_DIM, qT, zero)
        q2T = q2T_ref[hh].astype(F32)
        qsq = jnp.sum(q2T * q2T, axis=0, keepdims=True)
        ksq = jnp.where(col < tq, kmax_ref[hh][:, 0:1], kmax_ref[hh][:, HEAD_DIM:HEAD_DIM + 1])
        c_refs[hh][...] = jnp.sqrt(qsq * ksq) * SHIFT_MARGIN
        l_refs[hh][...] = jnp.zeros(l_refs[hh].shape, F32)
        acc_refs[hh][...] = jnp.zeros(acc_refs[hh].shape, F32)

    def scores(key0, nkeys):
        rows = pl.ds(pl.multiple_of(key0, nkeys), nkeys)
        return [_dot(k_ref[0, rows, hh * LANES:(hh + 1) * LANES], q2T_ref[hh])
                for hh in range(N_HEADS)]

    def diagonal_mask(nkeys):
        key = lax.broadcasted_iota(jnp.int32, (nkeys, 2 * tq), 0)
        qcol = lax.broadcasted_iota(jnp.int32, (nkeys, 2 * tq), 1)
        return key <= jnp.where(qcol >= tq, qcol - tq, qcol) + (nkeys - tq)

    def fixed_shift_tiles(tiles):
        sTs = [scores(key0, nkeys) for key0, nkeys, _, _ in tiles]
        for hh in range(N_HEADS):
            total, row_sum = None, None
            for (key0, nkeys, vT_of, diagonal), sT in zip(tiles, sTs):
                arg = sT[hh] - c_refs[hh][...]
                if diagonal:
                    arg = jnp.where(diagonal_mask(nkeys), arg, MASK_VALUE)
                p = jnp.exp2(arg)
                part = _dot(vT_of(hh), p.astype(BF16))
                p_sum = jnp.sum(p, axis=0, keepdims=True)
                total = part if total is None else total + part
                row_sum = p_sum if row_sum is None else row_sum + p_sum
            acc_refs[hh][...] += total
            l_refs[hh][...] += row_sum

    def running_max_tile(key0, nkeys, vT_of, diagonal):
        sTs = scores(key0, nkeys)
        pTs, alphas = [], []
        for hh in range(N_HEADS):
            sT = jnp.where(diagonal_mask(nkeys), sTs[hh], MASK_VALUE) if diagonal else sTs[hh]
            m_old = m_refs[hh][...]
            m_new = jnp.maximum(m_old, jnp.max(sT, axis=0, keepdims=True))
            m_refs[hh][...] = m_new
            alphas.append(jnp.exp2(m_old - m_new))
            p = jnp.exp2(sT - m_new)
            l_refs[hh][...] = alphas[hh] * l_refs[hh][...] + jnp.sum(p, axis=0, keepdims=True)
            pTs.append(p.astype(BF16))
        for hh in range(N_HEADS):
            acc_refs[hh][...] = alphas[hh] * acc_refs[hh][...] + _dot(vT_of(hh), pTs[hh])

    def full_tile(j):
        return (j * tk, tk, lambda hh: vT_ref[0, hh, j], False)

    def diagonal_tile(c):
        nkeys = (c + 1) * tq
        return (n_full * tk, nkeys, lambda hh: vT_ref[0, hh, n_full, :, 0:nkeys], True)

    def finalize():
        lam = (jnp.exp(jnp.sum(lq1_ref[...] * lk1_ref[...], keepdims=True))
               - jnp.exp(jnp.sum(lq2_ref[...] * lk2_ref[...], keepdims=True)) + lam_init)
        for hh in range(N_HEADS):
            oT = acc_refs[hh][...] / l_refs[hh][...]
            o = (oT[:, 0:tq] - lam * oT[:, tq:2 * tq]).T
            o_ref[0, :, hh * V_DIM:(hh + 1) * V_DIM] = (
                _rms(o, g_ref[...]) * (1.0 - lam_init)).astype(o_ref.dtype)

    per_trip = TILES_PER_TRIP[tq]

    def trip(p, carry):
        fixed_shift_tiles([full_tile(per_trip * p + t) for t in range(per_trip)])
        return carry

    lax.fori_loop(0, n_full // per_trip, trip, 0)
    for left in range(per_trip):
        for c in range(ratio):
            tiles = [full_tile(n_full - left + t) for t in range(left)] + [diagonal_tile(c)]
            pl.when(jnp.logical_and(n_full % per_trip == left, sub == c))(
                functools.partial(fixed_shift_tiles, tiles))
    finalize()

    smallest = jnp.min(jnp.concatenate([l_refs[hh][...] for hh in range(N_HEADS)], axis=1))

    @pl.when(jnp.logical_not(smallest >= MIN_ROW_SUM))
    def _():
        for hh in range(N_HEADS):
            m_refs[hh][...] = jnp.full(m_refs[hh].shape, MASK_VALUE, F32)
            l_refs[hh][...] = jnp.zeros(l_refs[hh].shape, F32)
            acc_refs[hh][...] = jnp.zeros(acc_refs[hh].shape, F32)

        def body(j, carry):
            running_max_tile(*full_tile(j))
            return carry

        lax.fori_loop(0, n_full, body, 0)
        for c in range(ratio):
            pl.when(sub == c)(functools.partial(running_max_tile, *diagonal_tile(c)))
        finalize()


def _attn(qT, k, vT, lq1, lk1, lq2, lk2, subln_g, *, tq, tk, lam_init):
    b, s, _ = k.shape
    nh = qT.shape[1]
    nk = s // tk
    row = lambda width: pl.BlockSpec((1, width), lambda bi, i: (0, 0))
    return pl.pallas_call(
        functools.partial(_attn_kernel, tq=tq, tk=tk, lam_init=lam_init),
        grid=(b, s // tq),
        in_specs=[row(HEAD_DIM), row(HEAD_DIM), row(HEAD_DIM), row(HEAD_DIM), row(V_DIM),
                  pl.BlockSpec((1, nh, 1, 2 * HEAD_DIM, tq), lambda bi, i: (bi, 0, i, 0, 0)),
                  pl.BlockSpec((1, s, nh * 2 * HEAD_DIM), lambda bi, i: (bi, 0, 0)),
                  pl.BlockSpec((1, nh, nk, V_DIM, tk), lambda bi, i: (bi, 0, 0, 0, 0))],
        out_specs=pl.BlockSpec((1, tq, nh * V_DIM), lambda bi, i: (bi, i, 0)),
        out_shape=jax.ShapeDtypeStruct((b, s, nh * V_DIM), BF16),
        scratch_shapes=([pltpu.VMEM((nh, 2 * HEAD_DIM, 2 * tq), BF16),
                         pltpu.VMEM((nh, 1, LANES), F32)]
                        + [pltpu.VMEM((1, 2 * tq), F32)] * (3 * nh)
                        + [pltpu.VMEM((V_DIM, 2 * tq), F32)] * nh),
        compiler_params=pltpu.CompilerParams(
            dimension_semantics=("arbitrary", "arbitrary"), vmem_limit_bytes=VMEM_LIMIT),
        name="attn",
    )(lq1, lk1, lq2, lk2, subln_g, qT, k, vT)


def _ffn_kernel(x_ref, ya_ref, yb_ref, yc_ref, wout_ref, g_ref, wg_ref, wu_ref, wd_ref, fg_ref,
                o_ref, act_ref, *, final):
    ca, diff = ya_ref.shape[1], yb_ref.shape[1]
    x = (x_ref[...] + _dot(ya_ref[...], wout_ref[0:ca, :])
         + _dot(yb_ref[...], wout_ref[ca:ca + diff, :])
         + _dot(yc_ref[...], wout_ref[ca + diff:, :]))
    h = _rms(x, g_ref[...]).astype(BF16)
    hidden = wg_ref.shape[1]
    for c0 in range(0, hidden, FFN_CHUNK):
        gate = _dot(h, wg_ref[:, c0:c0 + FFN_CHUNK])
        up = _dot(h, wu_ref[:, c0:c0 + FFN_CHUNK])
        act_ref[:, c0:c0 + FFN_CHUNK] = (gate * jax.nn.sigmoid(gate) * up).astype(BF16)
    y = x + _dot(act_ref[...], wd_ref[...])
    if final:
        y = _rms(y, fg_ref[...])
    o_ref[...] = y


def _ffn(x2d, ya, yb, yc, w_out, g, w_gate, w_up, w_down, final_g, *, layer, tm, final):
    t, d = x2d.shape
    hidden = w_gate.shape[2]
    tok = lambda width: pl.BlockSpec((tm, width), lambda i: (i, 0))
    const = lambda r, c: pl.BlockSpec((r, c), lambda i: (0, 0), pipeline_mode=pl.Buffered(1))
    weight = lambda r, c: pl.BlockSpec((None, r, c), lambda i: (layer, 0, 0),
                                       pipeline_mode=pl.Buffered(1))
    return pl.pallas_call(
        functools.partial(_ffn_kernel, final=final),
        grid=(t // tm,),
        in_specs=[tok(d), tok(ya.shape[1]), tok(yb.shape[1]), tok(yc.shape[1]), weight(d, d),
                  const(1, d), weight(d, hidden), weight(d, hidden), weight(hidden, d), const(1, d)],
        out_specs=tok(d),
        out_shape=jax.ShapeDtypeStruct((t, d), F32),
        scratch_shapes=[pltpu.VMEM((tm, hidden), BF16)],
        compiler_params=pltpu.CompilerParams(
            dimension_semantics=("arbitrary",), vmem_limit_bytes=VMEM_LIMIT),
        name="ffn",
    )(x2d, ya, yb, yc, w_out, g, w_gate, w_up, w_down, final_g)


def _rope_tables(positions):
    inv_freq = 1.0 / (ROPE_THETA ** (jnp.arange(0, HEAD_DIM, 2, dtype=F32) / HEAD_DIM))
    ang = positions.astype(F32)[:, None, :] * inv_freq[None, :, None]
    return jnp.cos(ang), jnp.sin(ang)


def _head_major(w, ca, diff):
    order = [m * N_HEADS + h for h in range(N_HEADS) for m in range(2)]
    q0 = 3 * ca

    def reorder(c0):
        return [w[..., c0 + blk * HEAD_DIM:c0 + (blk + 1) * HEAD_DIM] for blk in order]

    return jnp.concatenate(
        [w[..., :q0]] + reorder(q0) + reorder(q0 + diff) + [w[..., q0 + 2 * diff:]], axis=-1)


def kernel(x, positions, mix_norm_g, w_in, short_conv_w, glu_b, conf_dw_w, conf_dw_b, conf_ln_g, conf_ln_b, lam_q1, lam_k1, lam_q2, lam_k2, diff_subln_g, w_out, ffn_norm_g, w_gate, w_up, w_down, final_norm_g):
    b, s, d = x.shape
    depth = w_in.shape[0]
    ca, diff = d // 4, d // 2
    assert diff == 2 * N_HEADS * HEAD_DIM and N_HEADS * V_DIM == diff
    tm = min(TM_PROJ, s)
    tm_ffn = min(TM_FFN, b * s)
    tq, tk = min(TQ, s), min(TK, s)
    assert s % tm == 0 and s % tk == 0 and tk % tq == 0 and tm % tk == 0 and (b * s) % tm_ffn == 0

    cos_t, sin_t = _rope_tables(positions)
    w_in_b = _head_major(w_in, ca, diff).astype(BF16)
    w_out_b, w_gate_b, w_up_b, w_down_b = (w.astype(BF16) for w in (w_out, w_gate, w_up, w_down))
    row = lambda v: v.reshape(1, -1)
    flat = lambda a: a.reshape(b * s, a.shape[-1])
    for l in range(depth):
        lam_init = 0.8 - 0.6 * math.exp(-0.3 * l)
        ya, yc, qT, k, vT = _proj_in(
            x, row(mix_norm_g[l]), cos_t, sin_t, w_in_b, row(glu_b[l]), short_conv_w[l], conf_dw_w[l],
            row(conf_dw_b[l]), row(conf_ln_g[l]), row(conf_ln_b[l]), layer=l, tm=tm, tq=tq, tk=tk)
        yb = _attn(qT, k, vT, row(lam_q1[l]), row(lam_k1[l]), row(lam_q2[l]), row(lam_k2[l]),
                   row(diff_subln_g[l]), tq=tq, tk=tk, lam_init=lam_init)
        x = _ffn(flat(x), flat(ya), flat(yb), flat(yc), w_out_b, row(ffn_norm_g[l]),
                 w_gate_b, w_up_b, w_down_b, row(final_norm_g),
                 layer=l, tm=tm_ffn, final=(l == depth - 1)).reshape(b, s, d)
    return x
```
